```python
import jax, jax.numpy as jnp
from jax import lax
import numpy as np

D_MODEL = 1024
BATCH = 4
SEQ = 4096
DEPTH = 4

MIX_WIDTH = D_MODEL
N_GROUPS = 4
GROUP_WIDTH = MIX_WIDTH // N_GROUPS
EPS = 1e-6
ROPE_THETA = 10000.0

MLA_HEADS = 4
MLA_NOPE = 64
MLA_ROPE = 32
MLA_V = GROUP_WIDTH // MLA_HEADS
MLA_Q_LORA = 256
MLA_KV_LORA = 128
ATTN_BLOCK = 128

SG_HEADS = 4
SG_CHUNK = 128
SG_HALF = GROUP_WIDTH

HG_HEADS = 4
HG_KEY = 128
HG_VAL = GROUP_WIDTH // HG_HEADS
HG_CHUNK = 64

RET_HEADS = 4
RET_QK = 64
RET_V = GROUP_WIDTH // RET_HEADS
RET_CHUNK = 128

D_FF = 2816
PLE_DIM = 256

IN_SPLITS = (MLA_Q_LORA, MLA_KV_LORA, MLA_ROPE,
             SG_HALF, SG_HALF,
             HG_HEADS * HG_KEY, HG_HEADS * HG_KEY, HG_HEADS * HG_VAL, HG_HEADS * HG_VAL,
             RET_HEADS * RET_QK, RET_HEADS * RET_QK, RET_HEADS * RET_V, RET_HEADS * RET_V)
IN_WIDTH = sum(IN_SPLITS)

kernel_name = 'hybrid_parallel_mla_sgu_hgrn2_retention_macaron'


def rmsnorm(x, g):
    xf = x.astype(jnp.float32)
    y = xf * lax.rsqrt(jnp.mean(xf * xf, axis=-1, keepdims=True) + EPS)
    return (y * g.astype(jnp.float32)).astype(x.dtype)


def layernorm(x, g):
    xf = x.astype(jnp.float32)
    xc = xf - jnp.mean(xf, axis=-1, keepdims=True)
    y = xc * lax.rsqrt(jnp.mean(xc * xc, axis=-1, keepdims=True) + EPS)
    return (y * g.astype(jnp.float32)).astype(x.dtype)


def swiglu(x, w_gate, w_up, w_down):
    return (jax.nn.silu(x @ w_gate) * (x @ w_up)) @ w_down


def rope(x, positions):
    d = x.shape[-1]
    inv = ROPE_THETA ** (-jnp.arange(0, d, 2, dtype=jnp.float32) / d)
    ang = positions.astype(jnp.float32)[..., None] * inv
    cos = jnp.cos(ang)[:, :, None, :]
    sin = jnp.sin(ang)[:, :, None, :]
    x1, x2 = jnp.split(x.astype(jnp.float32), 2, axis=-1)
    return jnp.concatenate([x1 * cos - x2 * sin, x1 * sin + x2 * cos], axis=-1).astype(x.dtype)


def mla_mixer(c_q, c_kv, k_pe, positions, q_norm, w_uq, kv_norm, w_ukv):
    B, S, _ = c_q.shape
    q = (rmsnorm(c_q, q_norm) @ w_uq).reshape(B, S, MLA_HEADS, MLA_NOPE + MLA_ROPE)
    q = jnp.concatenate([q[..., :MLA_NOPE], rope(q[..., MLA_NOPE:], positions)], axis=-1)
    kv = (rmsnorm(c_kv, kv_norm) @ w_ukv).reshape(B, S, MLA_HEADS, MLA_NOPE + MLA_V)
    k_rot = rope(k_pe[:, :, None, :], positions)
    k = jnp.concatenate([kv[..., :MLA_NOPE],
                         jnp.broadcast_to(k_rot, (B, S, MLA_HEADS, MLA_ROPE))], axis=-1)
    v = kv[..., MLA_NOPE:]
    scale = (MLA_NOPE + MLA_ROPE) ** -0.5
    nb = S // ATTN_BLOCK
    qb = (q * scale).reshape(B, nb, ATTN_BLOCK, MLA_HEADS, MLA_NOPE + MLA_ROPE).transpose(1, 0, 3, 2, 4)
    kh = k.transpose(0, 2, 1, 3)
    vh = v.transpose(0, 2, 1, 3)
    key_idx = jnp.arange(S)

    def query_block(args):
        q_blk, bi = args
        s = jnp.einsum('bhqd,bhkd->bhqk', q_blk, kh).astype(jnp.float32)
        q_idx = bi * ATTN_BLOCK + jnp.arange(ATTN_BLOCK)
        s = jnp.where(key_idx[None, :] <= q_idx[:, None], s, -jnp.inf)
        pr = jax.nn.softmax(s, axis=-1).astype(vh.dtype)
        return jnp.einsum('bhqk,bhkd->bhqd', pr, vh)

    o = lax.map(query_block, (qb, jnp.arange(nb)))
    return o.transpose(1, 0, 3, 2, 4).reshape(B, S, MLA_HEADS * MLA_V)


def spatial_gating_mixer(z_u, z_v, ln_g, w_s, b_s):
    B, S, _ = z_u.shape
    u = jax.nn.gelu(z_u, approximate=False)
    v = layernorm(jax.nn.gelu(z_v, approximate=False), ln_g)
    n = S // SG_CHUNK
    vh = v.reshape(B, n, SG_CHUNK, SG_HEADS, SG_HALF // SG_HEADS)
    causal = jnp.tril(jnp.ones((SG_CHUNK, SG_CHUNK), dtype=bool))
    w = jnp.where(causal[None], w_s, 0.0).astype(v.dtype)
    mixed = jnp.einsum('hts,bnshd->bnthd', w, vh) + b_s.T[None, None, :, :, None]
    return u * mixed.reshape(B, S, SG_HALF)


def hgrn2_mixer(q, f, i, g, lb, norm_g):
    B, S, _ = q.shape
    f32 = jnp.float32
    forget = lb + (1.0 - lb) * jax.nn.sigmoid(f.astype(f32))
    log_f = jnp.log(forget)
    k = 1.0 - forget
    n = S // HG_CHUNK

    def chunks(t, d):
        return t.astype(f32).reshape(B, n, HG_CHUNK, HG_HEADS, d).transpose(1, 0, 3, 2, 4)

    qc, kc, gc = chunks(q, HG_KEY), chunks(k, HG_KEY), chunks(log_f, HG_KEY)
    vc = chunks(i, HG_VAL)
    causal = jnp.tril(jnp.ones((HG_CHUNK, HG_CHUNK), dtype=bool))[None, None, :, :, None]

    def step(state, inp):
        qt, kt, vt, gt = inp
        b = jnp.cumsum(gt, axis=2)
        diff = jnp.where(causal, b[:, :, :, None, :] - b[:, :, None, :, :], -jnp.inf)
        scores = jnp.einsum('bhtk,bhtsk,bhsk->bhts', qt, jnp.exp(diff), kt)
        o = (jnp.einsum('bhts,bhsv->bhtv', scores, vt)
             + jnp.einsum('bhtk,bhkv->bhtv', qt * jnp.exp(b), state))
        b_last = b[:, :, -1:, :]
        new_state = (jnp.exp(b_last[:, :, 0, :])[..., None] * state
                     + jnp.einsum('bhsk,bhsv->bhkv', kt * jnp.exp(b_last - b), vt))
        return new_state, o

    state0 = jnp.zeros((B, HG_HEADS, HG_KEY, HG_VAL), f32)
    _, o = lax.scan(step, state0, (qc, kc, vc, gc))
    o = o.transpose(1, 0, 3, 2, 4).reshape(B, S, HG_HEADS, HG_VAL)
    o = o * lax.rsqrt(jnp.mean(o * o, axis=-1, keepdims=True) + EPS) * norm_g.astype(f32).reshape(HG_HEADS, HG_VAL)
    gate = jax.nn.silu(g.astype(f32)).reshape(B, S, HG_HEADS, HG_VAL)
    return (o * gate).reshape(B, S, HG_HEADS * HG_VAL).astype(q.dtype)


def retention_mixer(q, k, v, g, positions, norm_g):
    B, S, _ = q.shape
    f32 = jnp.float32
    H, C = RET_HEADS, RET_CHUNK
    n = S // C
    qr = rope(q.reshape(B, S, H, RET_QK), positions).astype(f32)
    kr = rope(k.reshape(B, S, H, RET_QK), positions).astype(f32) * (RET_QK ** -0.5)
    vr = v.reshape(B, S, H, RET_V).astype(f32)

    def chunks(t):
        return t.reshape(B, n, C, H, t.shape[-1]).transpose(0, 3, 1, 2, 4)

    qh, kh, vh = chunks(qr), chunks(kr), chunks(vr)
    lg = jnp.log1p(-jnp.power(2.0, -(5.0 + jnp.arange(H, dtype=f32))))
    pos = jnp.arange(C, dtype=f32)
    rel = pos[:, None] - pos[None, :]
    decay = jnp.where(rel >= 0, jnp.exp(jnp.maximum(rel, 0.0)[None] * lg[:, None, None]), 0.0)
    scores = jnp.einsum('bhntd,bhnsd->bhnts', qh, kh) * decay[None, :, None]
    o = jnp.einsum('bhnts,bhnse->bhnte', scores, vh)
    k_dec = kh * jnp.exp((C - 1.0 - pos)[None, :] * lg[:, None])[None, :, None, :, None]
    kv = jnp.einsum('bhnsd,bhnse->bhnde', k_dec, vh)
    cidx = jnp.arange(n, dtype=f32)
    gap = cidx[:, None] - 1.0 - cidx[None, :]
    cross = jnp.where(gap >= 0, jnp.exp(C * jnp.maximum(gap, 0.0)[None] * lg[:, None, None]), 0.0)
    state = jnp.einsum('hnm,bhmde->bhnde', cross, kv)
    q_dec = qh * jnp.exp((pos + 1.0)[None, :] * lg[:, None])[None, :, None, :, None]
    o = o + jnp.einsum('bhntd,bhnde->bhnte', q_dec, state)
    o = o.transpose(0, 2, 3, 1, 4).reshape(B, S, H, RET_V)
    oc = o - jnp.mean(o, axis=-1, keepdims=True)
    o = oc * lax.rsqrt(jnp.mean(oc * oc, axis=-1, keepdims=True) + EPS) * norm_g.astype(f32).reshape(H, RET_V)
    gate = jax.nn.silu(g.astype(f32)).reshape(B, S, H, RET_V)
    return (o * gate).reshape(B, S, H * RET_V).astype(q.dtype)


def setup_inputs(seed: int = 0) -> dict:
    key = jax.random.key(seed)
    ks = iter(jax.random.split(key, 40))
    f32 = jnp.float32

    def nrm(shape, scale):
        return jax.random.normal(next(ks), shape, f32) * scale

    def gain(shape):
        return 1.0 + 0.01 * jax.random.normal(next(ks), shape, f32)

    L, D = DEPTH, D_MODEL
    return {
        'x': nrm((BATCH, SEQ, D), 1.0),
        'p': nrm((DEPTH, BATCH, SEQ, PLE_DIM), 1.0),
        'positions': jnp.broadcast_to(jnp.arange(SEQ, dtype=jnp.int32), (BATCH, SEQ)),
        'ffn1_norm': gain((L, D)),
        'ffn1_w_gate': nrm((L, D, D_FF), D ** -0.5),
        'ffn1_w_up': nrm((L, D, D_FF), D ** -0.5),
        'ffn1_w_down': nrm((L, D_FF, D), D_FF ** -0.5),
        'mix_norm': gain((L, D)),
        'w_in': nrm((L, D, IN_WIDTH), D ** -0.5),
        'mla_q_norm': gain((L, MLA_Q_LORA)),
        'mla_w_uq': nrm((L, MLA_Q_LORA, MLA_HEADS * (MLA_NOPE + MLA_ROPE)), MLA_Q_LORA ** -0.5),
        'mla_kv_norm': gain((L, MLA_KV_LORA)),
        'mla_w_ukv': nrm((L, MLA_KV_LORA, MLA_HEADS * (MLA_NOPE + MLA_V)), MLA_KV_LORA ** -0.5),
        'sg_ln': gain((L, SG_HALF)),
        'sg_w_s': nrm((L, SG_HEADS, SG_CHUNK, SG_CHUNK), 0.5 * SG_CHUNK ** -0.5),
        'sg_b_s': 1.0 + nrm((L, SG_HEADS, SG_CHUNK), 0.1),
        'hg_lb_logits': nrm((L, HG_HEADS * HG_KEY), 1.0),
        'hg_norm': gain((L, HG_HEADS * HG_VAL)),
        'ret_norm': gain((L, RET_HEADS * RET_V)),
        'w_out': nrm((L, MIX_WIDTH, D), MIX_WIDTH ** -0.5),
        'ffn2_norm': gain((L, D)),
        'ffn2_w_gate': nrm((L, D, D_FF), D ** -0.5),
        'ffn2_w_up': nrm((L, D, D_FF), D ** -0.5),
        'ffn2_w_down': nrm((L, D_FF, D), D_FF ** -0.5),
        'ple_norm': gain((L, D)),
        'ple_w_proj': nrm((L, PLE_DIM, D), PLE_DIM ** -0.5),
        'ple_w_gate': nrm((L, D, D), D ** -0.5),
        'final_norm': gain((D,)),
    }


def reference(x, p, positions, ffn1_norm, ffn1_w_gate, ffn1_w_up, ffn1_w_down,
              mix_norm, w_in, mla_q_norm, mla_w_uq, mla_kv_norm, mla_w_ukv,
              sg_ln, sg_w_s, sg_b_s, hg_lb_logits, hg_norm, ret_norm, w_out,
              ffn2_norm, ffn2_w_gate, ffn2_w_up, ffn2_w_down,
              ple_norm, ple_w_proj, ple_w_gate, final_norm):
    split_points = np.cumsum(IN_SPLITS)[:-1].tolist()
    lb_all = jnp.cumsum(jax.nn.softmax(hg_lb_logits.astype(jnp.float32), axis=0), axis=0)
    lb_all = lb_all - lb_all[0:1]
    h = x
    for l in range(DEPTH):
        h = h + 0.5 * swiglu(rmsnorm(h, ffn1_norm[l]), ffn1_w_gate[l], ffn1_w_up[l], ffn1_w_down[l])
        z = rmsnorm(h, mix_norm[l]) @ w_in[l]
        (c_q, c_kv, k_pe, z_u, z_v, hq, hf, hi, hg,
         rq, rk, rv, rg) = jnp.split(z, split_points, axis=-1)
        y_a = mla_mixer(c_q, c_kv, k_pe, positions, mla_q_norm[l], mla_w_uq[l], mla_kv_norm[l], mla_w_ukv[l])
        y_b = spatial_gating_mixer(z_u, z_v, sg_ln[l], sg_w_s[l], sg_b_s[l])
        y_c = hgrn2_mixer(hq, hf, hi, hg, lb_all[l], hg_norm[l])
        y_d = retention_mixer(rq, rk, rv, rg, positions, ret_norm[l])
        h = h + jnp.concatenate([y_a, y_b, y_c, y_d], axis=-1) @ w_out[l]
        h = h + 0.5 * swiglu(rmsnorm(h, ffn2_norm[l]), ffn2_w_gate[l], ffn2_w_up[l], ffn2_w_down[l])
        gate = jax.nn.sigmoid(rmsnorm(h, ple_norm[l]) @ ple_w_gate[l])
        h = h + (p[l] @ ple_w_proj[l]) * gate
    return rmsnorm(h, final_norm)
```

```python
import functools

import numpy as np
import jax
import jax.numpy as jnp
from jax import lax
from jax.experimental import pallas as pl
from jax.experimental.pallas import tpu as pltpu

F32 = jnp.float32
BF16 = jnp.bfloat16
EPS = 1e-6
ROPE_THETA = 10000.0
NEG_BIG = -1e30

D_MODEL = 1024
D_FF = 2816
FF_CHUNK = 256
N_FF_CHUNKS = D_FF // FF_CHUNK
PLE_DIM = 256
GROUP_WIDTH = 256
N_HEADS = 4
HEAD_V = GROUP_WIDTH // N_HEADS

MLA_NOPE, MLA_ROPE, MLA_Q_LORA, MLA_KV_LORA = 64, 32, 256, 128
MLA_HEAD_PAD = 128
SG_CHUNK = 128
HG_KEY = 128
HG_SUB = 8
HG_GROUP = 64
HG_TILE = 256
RET_QK = 64
RET_CHUNK = 256

ZM_W, ZS_W, ZH_W, ZR_W = 512, 512, 1536, 1024
Z_W = ZM_W + ZS_W + ZH_W + ZR_W

TOKEN_TILE = 512
ATTN_TILE = 256
VMEM_LIMIT = 56 * 1024 * 1024

NT_DIMS = (((1,), (1,)), ((), ()))
TN_DIMS = (((0,), (0,)), ((), ()))


def _rms(x, g):
    return x * lax.rsqrt(jnp.mean(x * x, axis=-1, keepdims=True) + EPS) * g


def _silu(x):
    return x * jax.nn.sigmoid(x)


def _dot(a, b):
    return jnp.dot(a, b, preferred_element_type=F32)


def _dot_f32(a, b):
    return jnp.dot(a, b, preferred_element_type=F32, precision=lax.Precision.HIGHEST)


def _params(*sem):
    return pltpu.CompilerParams(dimension_semantics=sem, vmem_limit_bytes=VMEM_LIMIT)


def _full(shape):
    return pl.BlockSpec(shape, lambda *_: (0,) * len(shape))


def _ffn_half_step(h, g_ref, wgu_ref, wd_ref, acc_ref):
    x = _rms(h, g_ref[...]).astype(BF16)
    acc_ref[...] = jnp.zeros_like(acc_ref)

    def body(c, carry):
        gu = _dot(x, wgu_ref[c])
        a = (_silu(gu[:, :FF_CHUNK]) * gu[:, FF_CHUNK:]).astype(BF16)
        acc_ref[...] += _dot(a, wd_ref[c])
        return carry

    lax.fori_loop(0, N_FF_CHUNKS, body, 0)
    return h + 0.5 * acc_ref[...]


def _pre_kernel(h_ref, g1_ref, wgu_ref, wd_ref, g2_ref, win_ref,
                h_out, zm_out, zs_out, zh_out, zr_out, acc_ref):
    h1 = _ffn_half_step(h_ref[...], g1_ref, wgu_ref, wd_ref, acc_ref)
    h_out[...] = h1
    x = _rms(h1, g2_ref[...]).astype(BF16)
    off = 0
    for out, w in ((zm_out, ZM_W), (zs_out, ZS_W), (zh_out, ZH_W), (zr_out, ZR_W)):
        out[...] = _dot(x, win_ref[:, off:off + w])
        off += w


def _pre_call(h, g1, wgu, wd, g2, win):
    T = h.shape[0]
    tm = TOKEN_TILE
    tok = lambda w: pl.BlockSpec((tm, w), lambda i: (i, 0))
    return pl.pallas_call(
        _pre_kernel,
        grid=(T // tm,),
        in_specs=[tok(D_MODEL), _full(g1.shape), _full(wgu.shape), _full(wd.shape),
                  _full(g2.shape), _full(win.shape)],
        out_specs=[tok(D_MODEL), tok(ZM_W), tok(ZS_W), tok(ZH_W), tok(ZR_W)],
        out_shape=[jax.ShapeDtypeStruct((T, w), F32) for w in (D_MODEL, ZM_W, ZS_W, ZH_W, ZR_W)],
        scratch_shapes=[pltpu.VMEM((tm, D_MODEL), F32)],
        compiler_params=_params("parallel"),
        name="pre_ffn_inproj",
    )(h, g1, wgu, wd, g2, win)


def _post_kernel(final, h_ref, ya_ref, yb_ref, yc_ref, yd_ref, p_ref, wo_ref, g_ref, wgu_ref,
                 wd_ref, gp_ref, wpp_ref, wpg_ref, gf_ref, out_ref, acc_ref):
    h = h_ref[...]
    for n, y_ref in enumerate((ya_ref, yb_ref, yc_ref, yd_ref)):
        h = h + _dot(y_ref[...], wo_ref[n * GROUP_WIDTH:(n + 1) * GROUP_WIDTH, :])
    h = _ffn_half_step(h, g_ref, wgu_ref, wd_ref, acc_ref)
    gate = jax.nn.sigmoid(_dot(_rms(h, gp_ref[...]).astype(BF16), wpg_ref[...]))
    h = h + _dot(p_ref[...].astype(BF16), wpp_ref[...]) * gate
    if final:
        h = _rms(h, gf_ref[...])
    out_ref[...] = h


def _post_call(h, ys, p, wo, g, wgu, wd, gp, wpp, wpg, gf, final):
    T = h.shape[0]
    tm = TOKEN_TILE
    tok = lambda w: pl.BlockSpec((tm, w), lambda i: (i, 0))
    return pl.pallas_call(
        functools.partial(_post_kernel, final),
        grid=(T // tm,),
        in_specs=[tok(D_MODEL)] + [tok(GROUP_WIDTH)] * 4 + [tok(PLE_DIM)]
        + [_full(a.shape) for a in (wo, g, wgu, wd, gp, wpp, wpg, gf)],
        out_specs=tok(D_MODEL),
        out_shape=jax.ShapeDtypeStruct((T, D_MODEL), F32),
        scratch_shapes=[pltpu.VMEM((tm, D_MODEL), F32)],
        compiler_params=_params("parallel"),
        name="post_outproj_ffn_ple",
    )(h, *ys, p, wo, g, wgu, wd, gp, wpp, wpg, gf)


def _rope_lanes(x, cb, su, sd):
    w = x.shape[1]
    half = MLA_ROPE // 2
    return x * cb + pltpu.roll(x, half, 1) * su + pltpu.roll(x, w - half, 1) * sd


def _mla_prep_kernel(z_ref, qn_ref, kvn_ref, wuq_ref, wukv_ref, cb_ref, su_ref, sd_ref,
                     q_out, k_out, v_out):
    z = z_ref[...]
    c_q = z[:, :MLA_Q_LORA]
    c_kv = z[:, MLA_Q_LORA:MLA_Q_LORA + MLA_KV_LORA]
    k_pe = z[:, MLA_Q_LORA + MLA_KV_LORA:]
    q = _dot(_rms(c_q, qn_ref[...]).astype(BF16), wuq_ref[...])
    kv = _dot(_rms(c_kv, kvn_ref[...]).astype(BF16), wukv_ref[...])
    cb, su, sd = cb_ref[...], su_ref[...], sd_ref[...]
    tile4 = lambda t: jnp.concatenate([t] * N_HEADS, axis=1)
    scale = (MLA_NOPE + MLA_ROPE) ** -0.5
    q_out[...] = (_rope_lanes(q, tile4(cb), tile4(su), tile4(sd)) * scale).astype(BF16)
    k_rot = _rope_lanes(k_pe, cb, su, sd)
    kw = N_HEADS * MLA_HEAD_PAD
    k_out[...] = (kv[:, :kw] + tile4(k_rot)).astype(BF16)
    v_out[...] = kv[:, kw:].astype(BF16)


def _mla_prep_call(zm, qn, kvn, wuq, wukv, cb, su, sd):
    T = zm.shape[0]
    tm = TOKEN_TILE
    kw = N_HEADS * MLA_HEAD_PAD
    tok = lambda w: pl.BlockSpec((tm, w), lambda i: (i, 0))
    return pl.pallas_call(
        _mla_prep_kernel,
        grid=(T // tm,),
        in_specs=[tok(ZM_W), _full(qn.shape), _full(kvn.shape), _full(wuq.shape), _full(wukv.shape),
                  tok(MLA_HEAD_PAD), tok(MLA_HEAD_PAD), tok(MLA_HEAD_PAD)],
        out_specs=[tok(kw)] * 3,
        out_shape=[jax.ShapeDtypeStruct((T, kw), BF16)] * 3,
        compiler_params=_params("parallel"),
        name="mla_prep",
    )(zm, qn, kvn, wuq, wukv, cb, su, sd)


def _attn_kernel(q_ref, k_ref, v_ref, o_ref):
    i = pl.program_id(1)
    tq = tk = ATTN_TILE
    row = lax.broadcasted_iota(jnp.int32, (tq, tk), 0)
    col = lax.broadcasted_iota(jnp.int32, (tq, tk), 1)
    causal = row >= col
    heads = []
    for h in range(N_HEADS):
        lanes = slice(h * MLA_HEAD_PAD, (h + 1) * MLA_HEAD_PAD)
        qh = q_ref[:, lanes]

        def step(j, carry, masked):
            m, l, acc = carry
            start = pl.multiple_of(j * tk, tk)
            kb = k_ref[pl.ds(start, tk), lanes]
            vb = v_ref[pl.ds(start, tk), lanes]
            s = lax.dot_general(qh, kb, NT_DIMS, preferred_element_type=F32)
            if masked:
                s = jnp.where(causal, s, -jnp.inf)
            m_new = jnp.maximum(m, jnp.max(s, axis=1, keepdims=True))
            p = jnp.exp(s - m_new)
            alpha = jnp.exp(m - m_new)
            l = alpha * l + jnp.sum(p, axis=1, keepdims=True)
            acc = alpha * acc + _dot(p.astype(BF16), vb)
            return m_new, l, acc

        carry = (jnp.full((tq, 1), -jnp.inf, F32), jnp.zeros((tq, 1), F32),
                 jnp.zeros((tq, MLA_HEAD_PAD), F32))
        carry = lax.fori_loop(0, i, functools.partial(step, masked=False), carry)
        _, l, acc = step(i, carry, True)
        heads.append(acc / l)
    lane = lax.broadcasted_iota(jnp.int32, (tq, MLA_HEAD_PAD), 1)
    lo = lane < HEAD_V
    pair = lambda a, b: jnp.where(lo, a, pltpu.roll(b, HEAD_V, 1))
    o_ref[...] = jnp.concatenate([pair(heads[0], heads[1]), pair(heads[2], heads[3])],
                                 axis=1).astype(o_ref.dtype)


def _attn_call(q, k, v, B, S):
    nq = S // ATTN_TILE
    kw = N_HEADS * MLA_HEAD_PAD
    return pl.pallas_call(
        _attn_kernel,
        grid=(B, nq),
        in_specs=[pl.BlockSpec((ATTN_TILE, kw), lambda b, i: (b * nq + i, 0)),
                  pl.BlockSpec((S, kw), lambda b, i: (b, 0)),
                  pl.BlockSpec((S, kw), lambda b, i: (b, 0))],
        out_specs=pl.BlockSpec((ATTN_TILE, GROUP_WIDTH), lambda b, i: (b * nq + i, 0)),
        out_shape=jax.ShapeDtypeStruct((B * S, GROUP_WIDTH), BF16),
        compiler_params=_params("parallel", "arbitrary"),
        name="mla_attention",
    )(q, k, v)


def _gelu(x):
    return 0.5 * x * (1.0 + lax.erf(x * np.float32(np.sqrt(0.5))))


def _sg_kernel(z_ref, ln_ref, w_ref, bias_ref, o_ref):
    z = z_ref[...]
    u = _gelu(z[:, :GROUP_WIDTH])
    gv = _gelu(z[:, GROUP_WIDTH:])
    xc = gv - jnp.mean(gv, axis=-1, keepdims=True)
    v = xc * lax.rsqrt(jnp.mean(xc * xc, axis=-1, keepdims=True) + EPS) * ln_ref[...]
    C = SG_CHUNK
    causal = (lax.broadcasted_iota(jnp.int32, (C, C), 0) >= lax.broadcasted_iota(jnp.int32, (C, C), 1))
    head = lax.broadcasted_iota(jnp.int32, (1, GROUP_WIDTH), 1) // HEAD_V
    ws = [jnp.where(causal, w_ref[h], 0.0).astype(BF16) for h in range(N_HEADS)]
    bias = bias_ref[...]
    for c in range(z.shape[0] // C):
        rows = slice(c * C, (c + 1) * C)
        vc = v[rows]
        mixed = bias
        for h in range(N_HEADS):
            mixed = mixed + _dot(ws[h], jnp.where(head == h, vc, 0.0).astype(BF16))
        o_ref[rows, :] = (u[rows] * mixed).astype(o_ref.dtype)


def _sg_call(zs, ln, w, bias):
    T = zs.shape[0]
    tm = TOKEN_TILE
    return pl.pallas_call(
        _sg_kernel,
        grid=(T // tm,),
        in_specs=[pl.BlockSpec((tm, ZS_W), lambda i: (i, 0)), _full(ln.shape), _full(w.shape),
                  _full(bias.shape)],
        out_specs=pl.BlockSpec((tm, GROUP_WIDTH), lambda i: (i, 0)),
        out_shape=jax.ShapeDtypeStruct((T, GROUP_WIDTH), BF16),
        compiler_params=_params("parallel"),
        name="spatial_gating",
    )(zs, ln, w, bias)


def _hg_constants():
    t = np.arange(HG_TILE)
    same = (t[:, None] // HG_GROUP) == (t[None, :] // HG_GROUP)
    m_cum = (same & (t[None, :] <= t[:, None])).astype(np.float32)
    m_sub = (same & (t[None, :] < (t[:, None] // HG_SUB) * HG_SUB)).astype(np.float32)
    m_last = same.astype(np.float32)
    delta = np.where(same, t[:, None] // HG_SUB - t[None, :] // HG_SUB, -1).astype(np.int32)
    kg = np.arange(N_HEADS * HG_KEY)
    e = np.arange(GROUP_WIDTH)
    ind = (kg[:, None] // HG_KEY == e[None, :] // HEAD_V).astype(np.float32)
    return (jnp.asarray(m_cum), jnp.asarray(m_sub), jnp.asarray(m_last), jnp.asarray(delta),
            jnp.asarray(ind, dtype=BF16))


def _head_avg():
    e = np.arange(GROUP_WIDTH)
    return jnp.asarray((e[:, None] // HEAD_V == e[None, :] // HEAD_V).astype(np.float32) / HEAD_V)


def _hg_kernel(z_ref, lb_ref, ng_ref, mcum_ref, msub_ref, mlast_ref, delta_ref, ind_ref, avg_ref,
               o_ref, st_ref):
    TT, KW = HG_TILE, N_HEADS * HG_KEY

    @pl.when(pl.program_id(1) == 0)
    def _():
        st_ref[...] = jnp.zeros_like(st_ref)

    z = z_ref[...]
    q = z[:, :KW]
    f = z[:, KW:2 * KW]
    vi = z[:, 2 * KW:2 * KW + GROUP_WIDTH]
    g = z[:, 2 * KW + GROUP_WIDTH:]
    lb = lb_ref[...]
    forget = lb + (1.0 - lb) * jax.nn.sigmoid(f)
    lf = jnp.log(forget)
    kk = 1.0 - forget
    b = _dot_f32(mcum_ref[...], lf)
    bsub = _dot_f32(msub_ref[...], lf)
    blast = _dot_f32(mlast_ref[...], lf)

    row = lax.broadcasted_iota(jnp.int32, (TT, 1), 0)
    r_sub = row & (HG_SUB - 1)
    i_sub = (row & (HG_GROUP - 1)) >> 3
    head_v = lax.broadcasted_iota(jnp.int32, (1, GROUP_WIDTH), 1) // HEAD_V
    ind = ind_ref[...]

    o = _dot((q * kk).astype(BF16), ind) * vi
    for d in range(1, HG_SUB):
        e = jnp.exp(jnp.where(r_sub >= d, b - pltpu.roll(b, d, 0), NEG_BIG))
        x = q * pltpu.roll(kk, d, 0) * e
        o = o + _dot(x.astype(BF16), ind) * pltpu.roll(vi, d, 0)

    qp = (q * jnp.exp(b - bsub)).astype(BF16)
    delta = delta_ref[...]
    a_off = [jnp.zeros((TT, TT), F32) for _ in range(N_HEADS)]
    n_sub = HG_GROUP // HG_SUB
    for off in range(1, n_sub):
        ref_b = pltpu.roll(bsub, TT - HG_SUB * off, 0)
        k_off = (kk * jnp.exp(jnp.where(i_sub + off < n_sub, ref_b - b, NEG_BIG))).astype(BF16)
        hit = delta == off
        for h in range(N_HEADS):
            lanes = slice(h * HG_KEY, (h + 1) * HG_KEY)
            a = lax.dot_general(qp[:, lanes], k_off[:, lanes], NT_DIMS, preferred_element_type=F32)
            a_off[h] = a_off[h] + jnp.where(hit, a, 0.0)
    for h in range(N_HEADS):
        o = o + _dot(a_off[h].astype(BF16), jnp.where(head_v == h, vi, 0.0).astype(BF16))

    qe = (q * jnp.exp(b)).astype(BF16)
    k_dec = (kk * jnp.exp(blast - b)).astype(BF16)
    e_last = jnp.exp(blast)
    vb = vi.astype(BF16)
    same_head = (lax.broadcasted_iota(jnp.int32, (GROUP_WIDTH, 1), 0) // HEAD_V
                 == lax.broadcasted_iota(jnp.int32, (1, KW), 1) // HG_KEY)
    st = st_ref[...]
    inter = []
    for gi in range(TT // HG_GROUP):
        rows = slice(gi * HG_GROUP, (gi + 1) * HG_GROUP)
        inter.append(lax.dot_general(qe[rows], st.astype(BF16), NT_DIMS, preferred_element_type=F32))
        upd = lax.dot_general(vb[rows], k_dec[rows], TN_DIMS, preferred_element_type=F32)
        st = st * e_last[gi * HG_GROUP:gi * HG_GROUP + 1, :] + jnp.where(same_head, upd, 0.0)
    st_ref[...] = st
    o = o + jnp.concatenate(inter, axis=0)

    ms = _dot_f32(o * o, avg_ref[...])
    o_ref[...] = (o * lax.rsqrt(ms + EPS) * ng_ref[...] * _silu(g)).astype(o_ref.dtype)


def _hg_call(zh, lb, ng, consts, avg, B, S):
    nt = S // HG_TILE
    ins = (lb, ng, *consts, avg)
    return pl.pallas_call(
        _hg_kernel,
        grid=(B, nt),
        in_specs=[pl.BlockSpec((HG_TILE, ZH_W), lambda b, i: (b * nt + i, 0))]
        + [_full(a.shape) for a in ins],
        out_specs=pl.BlockSpec((HG_TILE, GROUP_WIDTH), lambda b, i: (b * nt + i, 0)),
        out_shape=jax.ShapeDtypeStruct((B * S, GROUP_WIDTH), BF16),
        scratch_shapes=[pltpu.VMEM((GROUP_WIDTH, N_HEADS * HG_KEY), F32)],
        compiler_params=_params("parallel", "arbitrary"),
        name="hgrn2",
    )(zh, *ins)


def _ret_constants():
    C = RET_CHUNK
    lg = np.log1p(-np.power(2.0, -(5.0 + np.arange(N_HEADS, dtype=np.float64))))
    pos = np.arange(C, dtype=np.float64)
    rel = pos[:, None] - pos[None, :]
    decay = np.where(rel >= 0, np.exp(np.maximum(rel, 0.0)[None] * lg[:, None, None]), 0.0)
    lane = np.arange(GROUP_WIDTH)
    lg_qk = lg[(lane % 128) // (RET_QK // 2)]
    lg_v = lg[lane // HEAD_V]
    gq = np.exp((pos + 1.0)[:, None] * lg_qk[None, :])
    gk = np.exp((C - 1.0 - pos)[:, None] * lg_qk[None, :])
    gc = np.exp(C * lg_v)[None, :]
    f = lambda a: jnp.asarray(a.astype(np.float32))
    return f(decay), f(gq), f(gk), f(gc)


def _ret_kernel(z_ref, cos_ref, sin_ref, dec_ref, gq_ref, gk_ref, gc_ref, ng_ref, avg_ref,
                o_ref, st_ref):
    @pl.when(pl.program_id(1) == 0)
    def _():
        st_ref[...] = jnp.zeros_like(st_ref)

    W = GROUP_WIDTH
    z = z_ref[...]
    q1, q2 = z[:, 0:128], z[:, 128:256]
    k1, k2 = z[:, 256:384], z[:, 384:512]
    v = z[:, 2 * W:3 * W]
    g = z[:, 3 * W:]
    c, s = cos_ref[...], sin_ref[...]
    qr = jnp.concatenate([q1 * c - q2 * s, q1 * s + q2 * c], axis=1)
    kr = jnp.concatenate([k1 * c - k2 * s, k1 * s + k2 * c], axis=1) * (RET_QK ** -0.5)
    lane = lax.broadcasted_iota(jnp.int32, (1, W), 1)
    head_qk = (lane & 127) // (RET_QK // 2)
    head_v = lane // HEAD_V
    krb = kr.astype(BF16)
    vb = v.astype(BF16)

    st = st_ref[...]
    o = _dot((qr * gq_ref[...]).astype(BF16), st.astype(BF16))
    for h in range(N_HEADS):
        qm = jnp.where(head_qk == h, qr, 0.0).astype(BF16)
        sc = lax.dot_general(qm, krb, NT_DIMS, preferred_element_type=F32) * dec_ref[h]
        o = o + _dot(sc.astype(BF16), jnp.where(head_v == h, vb, jnp.zeros_like(vb)))

    upd = lax.dot_general((kr * gk_ref[...]).astype(BF16), vb, TN_DIMS, preferred_element_type=F32)
    row_head = (lax.broadcasted_iota(jnp.int32, (W, 1), 0) & 127) // (RET_QK // 2)
    st_ref[...] = st * gc_ref[...] + jnp.where(row_head == head_v, upd, 0.0)

    avg = avg_ref[...]
    oc = o - _dot_f32(o, avg)
    var = _dot_f32(oc * oc, avg)
    o_ref[...] = (oc * lax.rsqrt(var + EPS) * ng_ref[...] * _silu(g)).astype(o_ref.dtype)


def _ret_call(zr, cos, sin, consts, ng, avg, B, S):
    C = RET_CHUNK
    nc = S // C
    ins = (*consts, ng, avg)
    tok = lambda w: pl.BlockSpec((C, w), lambda b, i: (b * nc + i, 0))
    return pl.pallas_call(
        _ret_kernel,
        grid=(B, nc),
        in_specs=[tok(ZR_W), tok(128), tok(128)] + [_full(a.shape) for a in ins],
        out_specs=tok(GROUP_WIDTH),
        out_shape=jax.ShapeDtypeStruct((B * S, GROUP_WIDTH), BF16),
        scratch_shapes=[pltpu.VMEM((GROUP_WIDTH, GROUP_WIDTH), F32)],
        compiler_params=_params("parallel", "arbitrary"),
        name="retention",
    )(zr, cos, sin, *ins)


def _prep_w_in(w):
    cuts = np.cumsum([0, 256, 128, 32, 256, 256, 512, 512, 256, 256, 256, 256, 256, 256])
    c_q, c_kv, k_pe, z_u, z_v, hq, hf, hi, hg, rq, rk, rv, rg = (
        w[:, cuts[n]:cuts[n + 1]] for n in range(13))
    zeros = lambda n: jnp.zeros((w.shape[0], n), w.dtype)
    half = MLA_ROPE // 2
    mla = [c_q, c_kv, zeros(MLA_NOPE), k_pe[:, :half], k_pe[:, half:], zeros(MLA_HEAD_PAD - MLA_NOPE - MLA_ROPE)]

    def split_halves(m):
        return m.reshape(-1, N_HEADS, 2, RET_QK // 2).transpose(0, 2, 1, 3).reshape(-1, N_HEADS * RET_QK)

    cols = mla + [z_u, z_v, hq, hf, hi, hg, split_halves(rq), split_halves(rk), rv, rg]
    return jnp.concatenate(cols, axis=1).astype(BF16)


def _prep_w_uq(w):
    w = w.reshape(MLA_Q_LORA, N_HEADS, MLA_NOPE + MLA_ROPE)
    w = jnp.pad(w, ((0, 0), (0, 0), (0, MLA_HEAD_PAD - MLA_NOPE - MLA_ROPE)))
    return w.reshape(MLA_Q_LORA, N_HEADS * MLA_HEAD_PAD).astype(BF16)


def _prep_w_ukv(w):
    w = w.reshape(MLA_KV_LORA, N_HEADS, MLA_NOPE + HEAD_V)
    pad = lambda m: jnp.pad(m, ((0, 0), (0, 0), (0, MLA_HEAD_PAD - m.shape[-1]))).reshape(MLA_KV_LORA, -1)
    return jnp.concatenate([pad(w[..., :MLA_NOPE]), pad(w[..., MLA_NOPE:])], axis=1).astype(BF16)


def _prep_ffn(w_gate, w_up, w_down):
    chunk = lambda w: w.reshape(D_MODEL, N_FF_CHUNKS, FF_CHUNK).transpose(1, 0, 2)
    wgu = jnp.concatenate([chunk(w_gate), chunk(w_up)], axis=-1).astype(BF16)
    wd = w_down.reshape(N_FF_CHUNKS, FF_CHUNK, D_MODEL).astype(BF16)
    return wgu, wd


def _rope_tables(positions):
    pos = positions.astype(F32).reshape(-1)[:, None]

    def cos_sin(d):
        inv = ROPE_THETA ** (-jnp.arange(0, d, 2, dtype=F32) / d)
        ang = pos * inv[None, :]
        return jnp.cos(ang), jnp.sin(ang)

    c, s = cos_sin(MLA_ROPE)
    T = pos.shape[0]
    z = lambda n: jnp.zeros((T, n), F32)
    tail = MLA_HEAD_PAD - MLA_NOPE - MLA_ROPE
    cb = jnp.concatenate([jnp.ones((T, MLA_NOPE), F32), c, c, z(tail)], axis=1)
    su = jnp.concatenate([z(MLA_NOPE), z(MLA_ROPE // 2), s, z(tail)], axis=1)
    sd = jnp.concatenate([z(MLA_NOPE), -s, z(MLA_ROPE // 2), z(tail)], axis=1)
    cr, sr = cos_sin(RET_QK)
    return cb, su, sd, jnp.tile(cr, (1, N_HEADS)), jnp.tile(sr, (1, N_HEADS))


def kernel(x, p, positions, ffn1_norm, ffn1_w_gate, ffn1_w_up, ffn1_w_down, mix_norm, w_in, mla_q_norm, mla_w_uq, mla_kv_norm, mla_w_ukv, sg_ln, sg_w_s, sg_b_s, hg_lb_logits, hg_norm, ret_norm, w_out, ffn2_norm, ffn2_w_gate, ffn2_w_up, ffn2_w_down, ple_norm, ple_w_proj, ple_w_gate, final_norm):
    B, S, D = x.shape
    L = w_in.shape[0]
    T = B * S
    row = lambda a: a.reshape(1, -1).astype(F32)

    cb, su, sd, cos_r, sin_r = _rope_tables(positions)
    hg_consts = _hg_constants()
    ret_consts = _ret_constants()
    avg = _head_avg()
    lb_all = jnp.cumsum(jax.nn.softmax(hg_lb_logits.astype(F32), axis=0), axis=0)
    lb_all = lb_all - lb_all[0:1]
    gf = row(final_norm)

    h = x.reshape(T, D)
    for l in range(L):
        wgu1, wd1 = _prep_ffn(ffn1_w_gate[l], ffn1_w_up[l], ffn1_w_down[l])
        wgu2, wd2 = _prep_ffn(ffn2_w_gate[l], ffn2_w_up[l], ffn2_w_down[l])
        h, zm, zs, zh, zr = _pre_call(h, row(ffn1_norm[l]), wgu1, wd1, row(mix_norm[l]), _prep_w_in(w_in[l]))

        q, k, v = _mla_prep_call(zm, row(mla_q_norm[l]), row(mla_kv_norm[l]), _prep_w_uq(mla_w_uq[l]),
                                 _prep_w_ukv(mla_w_ukv[l]), cb, su, sd)
        y_a = _attn_call(q, k, v, B, S)
        sg_bias = jnp.repeat(sg_b_s[l].T, HEAD_V, axis=1)
        y_b = _sg_call(zs, row(sg_ln[l]), sg_w_s[l], sg_bias)
        y_c = _hg_call(zh, row(lb_all[l]), row(hg_norm[l]), hg_consts, avg, B, S)
        y_d = _ret_call(zr, cos_r, sin_r, ret_consts, row(ret_norm[l]), avg, B, S)

        h = _post_call(h, (y_a, y_b, y_c, y_d), p[l].reshape(T, PLE_DIM), w_out[l].astype(BF16),
                       row(ffn2_norm[l]), wgu2, wd2, row(ple_norm[l]), ple_w_proj[l].astype(BF16),
                       ple_w_gate[l].astype(BF16), gf, final=(l == L - 1))
    return h.reshape(B, S, D)
```

```python
import functools

import numpy as np
import jax
import jax.numpy as jnp
from jax import lax
from jax.experimental import pallas as pl
from jax.experimental.pallas import tpu as pltpu

F32 = jnp.float32
BF16 = jnp.bfloat16
EPS = 1e-6
ROPE_THETA = 10000.0

D_MODEL = 1024
D_FF = 2816
FF_CHUNK = 256
N_FF_CHUNKS = D_FF // FF_CHUNK
PLE_DIM = 256
GROUP_WIDTH = 256
N_HEADS = 4
HEAD_V = GROUP_WIDTH // N_HEADS

MLA_NOPE, MLA_ROPE, MLA_Q_LORA, MLA_KV_LORA = 64, 32, 256, 128
MLA_HEAD_PAD = 128
SG_CHUNK = 128
HG_KEY = 128
HG_SUB = 8
HG_GROUP = 64
HG_TILE = 256
RET_QK = 64
RET_CHUNK = 256

ZM_W, ZS_W, ZH_W, ZR_W = 512, 512, 1536, 1024
Z_W = ZM_W + ZS_W + ZH_W + ZR_W

TOKEN_TILE = 512
ATTN_TILE = 512
VMEM_LIMIT = 56 * 1024 * 1024

NT_DIMS = (((1,), (1,)), ((), ()))
TN_DIMS = (((0,), (0,)), ((), ()))


def _rms(x, g):
    return x * lax.rsqrt(jnp.mean(x * x, axis=-1, keepdims=True) + EPS) * g


def _silu(x):
    return x * jax.nn.sigmoid(x)


def _dot(a, b):
    return jnp.dot(a, b, preferred_element_type=F32)


def _params(*sem):
    return pltpu.CompilerParams(dimension_semantics=sem, vmem_limit_bytes=VMEM_LIMIT)


def _full(shape):
    return pl.BlockSpec(shape, lambda *_: (0,) * len(shape))


def _ffn_half_step(h, g_ref, wg_ref, wu_ref, wd_ref):
    x = _rms(h, g_ref[...]).astype(BF16)
    acc = jnp.zeros_like(h)
    for c in range(N_FF_CHUNKS):
        cols = slice(c * FF_CHUNK, (c + 1) * FF_CHUNK)
        a = (_silu(_dot(x, wg_ref[:, cols])) * _dot(x, wu_ref[:, cols])).astype(BF16)
        acc = acc + _dot(a, wd_ref[cols, :])
    return h + 0.5 * acc


def _pre_kernel(h_ref, g1_ref, wg_ref, wu_ref, wd_ref, g2_ref, win_ref,
                h_out, zm_out, zs_out, zh_out, zr_out):
    h1 = _ffn_half_step(h_ref[...], g1_ref, wg_ref, wu_ref, wd_ref)
    h_out[...] = h1
    x = _rms(h1, g2_ref[...]).astype(BF16)
    off = 0
    for out, w in ((zm_out, ZM_W), (zs_out, ZS_W), (zh_out, ZH_W), (zr_out, ZR_W)):
        out[...] = _dot(x, win_ref[:, off:off + w])
        off += w


def _pre_call(h, g1, wg, wu, wd, g2, win):
    T = h.shape[0]
    tm = TOKEN_TILE
    tok = lambda w: pl.BlockSpec((tm, w), lambda i: (i, 0))
    return pl.pallas_call(
        _pre_kernel,
        grid=(T // tm,),
        in_specs=[tok(D_MODEL)] + [_full(a.shape) for a in (g1, wg, wu, wd, g2, win)],
        out_specs=[tok(D_MODEL), tok(ZM_W), tok(ZS_W), tok(ZH_W), tok(ZR_W)],
        out_shape=[jax.ShapeDtypeStruct((T, w), F32) for w in (D_MODEL, ZM_W, ZS_W, ZH_W, ZR_W)],
        compiler_params=_params("parallel"),
        name="pre_ffn_inproj",
    )(h, g1, wg, wu, wd, g2, win)


def _post_kernel(final, h_ref, ya_ref, yb_ref, yc_ref, yd_ref, p_ref, wo_ref, g_ref, wg_ref, wu_ref,
                 wd_ref, gp_ref, wpp_ref, wpg_ref, gf_ref, out_ref):
    h = h_ref[...]
    for n, y_ref in enumerate((ya_ref, yb_ref, yc_ref, yd_ref)):
        h = h + _dot(y_ref[...], wo_ref[n * GROUP_WIDTH:(n + 1) * GROUP_WIDTH, :])
    h = _ffn_half_step(h, g_ref, wg_ref, wu_ref, wd_ref)
    gate = jax.nn.sigmoid(_dot(_rms(h, gp_ref[...]).astype(BF16), wpg_ref[...]))
    h = h + _dot(p_ref[...].astype(BF16), wpp_ref[...]) * gate
    if final:
        h = _rms(h, gf_ref[...])
    out_ref[...] = h


def _post_call(h, ys, p, wo, g, wg, wu, wd, gp, wpp, wpg, gf, final):
    T = h.shape[0]
    tm = TOKEN_TILE
    tok = lambda w: pl.BlockSpec((tm, w), lambda i: (i, 0))
    return pl.pallas_call(
        functools.partial(_post_kernel, final),
        grid=(T // tm,),
        in_specs=[tok(D_MODEL)] + [tok(GROUP_WIDTH)] * 4 + [tok(PLE_DIM)]
        + [_full(a.shape) for a in (wo, g, wg, wu, wd, gp, wpp, wpg, gf)],
        out_specs=tok(D_MODEL),
        out_shape=jax.ShapeDtypeStruct((T, D_MODEL), F32),
        compiler_params=_params("parallel"),
        name="post_outproj_ffn_ple",
    )(h, *ys, p, wo, g, wg, wu, wd, gp, wpp, wpg, gf)


def _rope_lanes(x, cb, su, sd):
    w = x.shape[1]
    half = MLA_ROPE // 2
    return x * cb + pltpu.roll(x, half, 1) * su + pltpu.roll(x, w - half, 1) * sd


def _mla_prep_kernel(z_ref, qn_ref, kvn_ref, wuq_ref, wukv_ref, cb_ref, su_ref, sd_ref,
                     q_out, k_out, v_out):
    z = z_ref[...]
    c_q = z[:, :MLA_Q_LORA]
    c_kv = z[:, MLA_Q_LORA:MLA_Q_LORA + MLA_KV_LORA]
    k_pe = z[:, MLA_Q_LORA + MLA_KV_LORA:]
    q = _dot(_rms(c_q, qn_ref[...]).astype(BF16), wuq_ref[...])
    kv = _dot(_rms(c_kv, kvn_ref[...]).astype(BF16), wukv_ref[...])
    cb, su, sd = cb_ref[...], su_ref[...], sd_ref[...]
    tile4 = lambda t: jnp.concatenate([t] * N_HEADS, axis=1)
    scale = (MLA_NOPE + MLA_ROPE) ** -0.5
    q_out[...] = (_rope_lanes(q, tile4(cb), tile4(su), tile4(sd)) * scale).astype(BF16)
    k_rot = _rope_lanes(k_pe, cb, su, sd)
    kw = N_HEADS * MLA_HEAD_PAD
    k_out[...] = (kv[:, :kw] + tile4(k_rot)).astype(BF16)
    v_out[...] = kv[:, kw:].astype(BF16)


def _mla_prep_call(zm, qn, kvn, wuq, wukv, cb, su, sd):
    T = zm.shape[0]
    tm = TOKEN_TILE
    kw = N_HEADS * MLA_HEAD_PAD
    tok = lambda w: pl.BlockSpec((tm, w), lambda i: (i, 0))
    return pl.pallas_call(
        _mla_prep_kernel,
        grid=(T // tm,),
        in_specs=[tok(ZM_W), _full(qn.shape), _full(kvn.shape), _full(wuq.shape), _full(wukv.shape),
                  tok(MLA_HEAD_PAD), tok(MLA_HEAD_PAD), tok(MLA_HEAD_PAD)],
        out_specs=[tok(kw)] * 3,
        out_shape=[jax.ShapeDtypeStruct((T, kw), BF16)] * 3,
        compiler_params=_params("parallel"),
        name="mla_prep",
    )(zm, qn, kvn, wuq, wukv, cb, su, sd)


def _attn_kernel(q_ref, k_ref, v_ref, o_ref):
    i = pl.program_id(1)
    tq = tk = ATTN_TILE
    row = lax.broadcasted_iota(jnp.int32, (tq, tk), 0)
    col = lax.broadcasted_iota(jnp.int32, (tq, tk), 1)
    causal = row >= col
    head_lanes = [slice(h * MLA_HEAD_PAD, (h + 1) * MLA_HEAD_PAD) for h in range(N_HEADS)]

    def step(j, carry, masked):
        start = pl.multiple_of(j * tk, tk)
        out = []
        for lanes, (m, l, acc) in zip(head_lanes, carry):
            kb = k_ref[pl.ds(start, tk), lanes]
            vb = v_ref[pl.ds(start, tk), lanes]
            s = lax.dot_general(q_ref[:, lanes], kb, NT_DIMS, preferred_element_type=F32)
            if masked:
                s = jnp.where(causal, s, -jnp.inf)
            m_new = jnp.maximum(m, jnp.max(s, axis=1, keepdims=True))
            p = jnp.exp(s - m_new)
            alpha = jnp.exp(m - m_new)
            l = alpha * l + jnp.sum(p, axis=1, keepdims=True)
            acc = alpha * acc + _dot(p.astype(BF16), vb)
            out.append((m_new, l, acc))
        return tuple(out)

    init = tuple((jnp.full((tq, 1), -jnp.inf, F32), jnp.zeros((tq, 1), F32),
                  jnp.zeros((tq, MLA_HEAD_PAD), F32)) for _ in range(N_HEADS))
    carry = lax.fori_loop(0, i, functools.partial(step, masked=False), init)
    carry = step(i, carry, True)
    heads = [acc / l for _, l, acc in carry]
    lane = lax.broadcasted_iota(jnp.int32, (tq, MLA_HEAD_PAD), 1)
    lo = lane < HEAD_V
    pair = lambda a, b: jnp.where(lo, a, pltpu.roll(b, HEAD_V, 1))
    o_ref[...] = jnp.concatenate([pair(heads[0], heads[1]), pair(heads[2], heads[3])],
                                 axis=1).astype(o_ref.dtype)


def _attn_call(q, k, v, B, S):
    nq = S // ATTN_TILE
    kw = N_HEADS * MLA_HEAD_PAD
    return pl.pallas_call(
        _attn_kernel,
        grid=(B, nq),
        in_specs=[pl.BlockSpec((ATTN_TILE, kw), lambda b, i: (b * nq + i, 0)),
                  pl.BlockSpec((S, kw), lambda b, i: (b, 0)),
                  pl.BlockSpec((S, kw), lambda b, i: (b, 0))],
        out_specs=pl.BlockSpec((ATTN_TILE, GROUP_WIDTH), lambda b, i: (b * nq + i, 0)),
        out_shape=jax.ShapeDtypeStruct((B * S, GROUP_WIDTH), BF16),
        compiler_params=_params("parallel", "arbitrary"),
        name="mla_attention",
    )(q, k, v)


def _gelu(x):
    return 0.5 * x * (1.0 + lax.erf(x * np.float32(np.sqrt(0.5))))


def _sg_kernel(z_ref, ln_ref, w_ref, bias_ref, o_ref):
    z = z_ref[...]
    u = _gelu(z[:, :GROUP_WIDTH])
    gv = _gelu(z[:, GROUP_WIDTH:])
    xc = gv - jnp.mean(gv, axis=-1, keepdims=True)
    v = xc * lax.rsqrt(jnp.mean(xc * xc, axis=-1, keepdims=True) + EPS) * ln_ref[...]
    C = SG_CHUNK
    causal = (lax.broadcasted_iota(jnp.int32, (C, C), 0) >= lax.broadcasted_iota(jnp.int32, (C, C), 1))
    head = lax.broadcasted_iota(jnp.int32, (1, GROUP_WIDTH), 1) // HEAD_V
    ws = [jnp.where(causal, w_ref[h], 0.0).astype(BF16) for h in range(N_HEADS)]
    bias = bias_ref[...]
    for c in range(z.shape[0] // C):
        rows = slice(c * C, (c + 1) * C)
        vc = v[rows]
        mixed = bias
        for h in range(N_HEADS):
            mixed = mixed + _dot(ws[h], jnp.where(head == h, vc, 0.0).astype(BF16))
        o_ref[rows, :] = (u[rows] * mixed).astype(o_ref.dtype)


def _sg_call(zs, ln, w, bias):
    T = zs.shape[0]
    tm = TOKEN_TILE
    return pl.pallas_call(
        _sg_kernel,
        grid=(T // tm,),
        in_specs=[pl.BlockSpec((tm, ZS_W), lambda i: (i, 0)), _full(ln.shape), _full(w.shape),
                  _full(bias.shape)],
        out_specs=pl.BlockSpec((tm, GROUP_WIDTH), lambda i: (i, 0)),
        out_shape=jax.ShapeDtypeStruct((T, GROUP_WIDTH), BF16),
        compiler_params=_params("parallel"),
        name="spatial_gating",
    )(zs, ln, w, bias)


def _split3(x):
    p1 = x.astype(BF16)
    r = x - p1.astype(F32)
    p2 = r.astype(BF16)
    p3 = (r - p2.astype(F32)).astype(BF16)
    return p1, p2, p3


def _sum_rows_f32(m3_ref, x):
    return _dot(m3_ref[...], jnp.concatenate(_split3(x), axis=0))


def _sum_lanes_f32(x, m3_ref):
    return _dot(jnp.concatenate(_split3(x), axis=1), m3_ref[...])


def _hg_constants():
    t = np.arange(HG_TILE)
    same = (t[:, None] // HG_GROUP) == (t[None, :] // HG_GROUP)
    m_cum = same & (t[None, :] <= t[:, None])
    m_sub = same & (t[None, :] < (t[:, None] // HG_SUB) * HG_SUB)
    m3 = np.tile(np.concatenate([m_cum, m_sub, same], axis=0), (1, 3)).astype(np.float32)
    kg = np.arange(N_HEADS * HG_KEY)
    e = np.arange(GROUP_WIDTH)
    ind = (kg[:, None] // HG_KEY == e[None, :] // HEAD_V).astype(np.float32)
    return jnp.asarray(m3, dtype=BF16), jnp.asarray(ind, dtype=BF16)


def _head_avg3():
    e = np.arange(GROUP_WIDTH)
    avg = (e[:, None] // HEAD_V == e[None, :] // HEAD_V).astype(np.float32) / HEAD_V
    return jnp.asarray(np.tile(avg, (3, 1)), dtype=BF16)


def _hg_kernel(z_ref, lb_ref, ng_ref, m3_ref, ind_ref, avg3_ref, o_ref, st_ref):
    TT, KW = HG_TILE, N_HEADS * HG_KEY
    n_grp, n_sub = TT // HG_GROUP, HG_GROUP // HG_SUB

    @pl.when(pl.program_id(1) == 0)
    def _():
        st_ref[...] = jnp.zeros_like(st_ref)

    z = z_ref[...]
    q = z[:, :KW]
    f = z[:, KW:2 * KW]
    vi = z[:, 2 * KW:2 * KW + GROUP_WIDTH]
    g = z[:, 2 * KW + GROUP_WIDTH:]
    lb = lb_ref[...]
    forget = lb + (1.0 - lb) * jax.nn.sigmoid(f)
    kk = 1.0 - forget
    sums = _sum_rows_f32(m3_ref, jnp.log(forget))
    b = sums[:TT]
    bsub = sums[TT:2 * TT]
    blast = sums[2 * TT:]
    head_v = lax.broadcasted_iota(jnp.int32, (1, GROUP_WIDTH), 1) // HEAD_V
    ind = ind_ref[...]

    tiles = lambda a: a.reshape(TT // HG_SUB, HG_SUB, a.shape[-1])
    q3, f3, v3 = tiles(q), tiles(forget), tiles(vi)
    r_sub = lax.broadcasted_iota(jnp.int32, (1, HG_SUB, 1), 1)
    o = _dot((q * kk).astype(BF16), ind) * vi
    decay, f_prev = None, f3
    for d in range(1, HG_SUB):
        f_d = pltpu.roll(f3, d, 1)
        decay = f3 if d == 1 else decay * f_prev
        x = jnp.where(r_sub >= d, q3 * (1.0 - f_d) * decay, 0.0)
        w = _dot(x.reshape(TT, KW).astype(BF16), ind)
        o = o + w * pltpu.roll(v3, d, 1).reshape(TT, GROUP_WIDTH)
        f_prev = f_d

    groups = lambda a: a.reshape(n_grp, n_sub, HG_SUB, a.shape[-1])
    qp4, kk4, b4, bsub4 = groups(q * jnp.exp(b - bsub)), groups(kk), groups(b), groups(bsub)
    zeros4 = lambda n: [jnp.zeros((n_grp, n, HG_SUB, KW), F32)] if n else []
    q_slots, k_slots = [], []
    for i in range(1, n_sub):
        k_i = kk4[:, :i] * jnp.exp(bsub4[:, i:i + 1] - b4[:, :i])
        k_slots.append(jnp.concatenate([k_i] + zeros4(n_sub - i), axis=1).reshape(TT, KW))
        q_slots.append(jnp.concatenate(zeros4(i) + [qp4[:, i:i + 1]] + zeros4(n_sub - 1 - i),
                                       axis=1).reshape(TT, KW))
    same_group = ((lax.broadcasted_iota(jnp.int32, (TT, TT), 0) // HG_GROUP)
                  == (lax.broadcasted_iota(jnp.int32, (TT, TT), 1) // HG_GROUP))
    for h in range(N_HEADS):
        lanes = slice(h * HG_KEY, (h + 1) * HG_KEY)
        lhs = jnp.concatenate([s[:, lanes] for s in q_slots], axis=1).astype(BF16)
        rhs = jnp.concatenate([s[:, lanes] for s in k_slots], axis=1).astype(BF16)
        a = lax.dot_general(lhs, rhs, NT_DIMS, preferred_element_type=F32)
        a = jnp.where(same_group, a, 0.0).astype(BF16)
        o = o + _dot(a, jnp.where(head_v == h, vi, 0.0).astype(BF16))

    qe = (q * jnp.exp(b)).astype(BF16)
    k_dec = (kk * jnp.exp(blast - b)).astype(BF16)
    e_last = jnp.exp(blast)
    vb = vi.astype(BF16)
    same_head = (lax.broadcasted_iota(jnp.int32, (GROUP_WIDTH, 1), 0) // HEAD_V
                 == lax.broadcasted_iota(jnp.int32, (1, KW), 1) // HG_KEY)
    st = st_ref[...]
    inter = []
    for gi in range(n_grp):
        rows = slice(gi * HG_GROUP, (gi + 1) * HG_GROUP)
        inter.append(lax.dot_general(qe[rows], st.astype(BF16), NT_DIMS, preferred_element_type=F32))
        upd = lax.dot_general(vb[rows], k_dec[rows], TN_DIMS, preferred_element_type=F32)
        st = st * e_last[gi * HG_GROUP:gi * HG_GROUP + 1, :] + jnp.where(same_head, upd, 0.0)
    st_ref[...] = st
    o = o + jnp.concatenate(inter, axis=0)

    ms = _sum_lanes_f32(o * o, avg3_ref)
    o_ref[...] = (o * lax.rsqrt(ms + EPS) * ng_ref[...] * _silu(g)).astype(o_ref.dtype)


def _hg_call(zh, lb, ng, consts, avg3, B, S):
    nt = S // HG_TILE
    ins = (lb, ng, *consts, avg3)
    return pl.pallas_call(
        _hg_kernel,
        grid=(B, nt),
        in_specs=[pl.BlockSpec((HG_TILE, ZH_W), lambda b, i: (b * nt + i, 0))]
        + [_full(a.shape) for a in ins],
        out_specs=pl.BlockSpec((HG_TILE, GROUP_WIDTH), lambda b, i: (b * nt + i, 0)),
        out_shape=jax.ShapeDtypeStruct((B * S, GROUP_WIDTH), BF16),
        scratch_shapes=[pltpu.VMEM((GROUP_WIDTH, N_HEADS * HG_KEY), F32)],
        compiler_params=_params("parallel", "arbitrary"),
        name="hgrn2",
    )(zh, *ins)


def _ret_constants():
    C = RET_CHUNK
    lg = np.log1p(-np.power(2.0, -(5.0 + np.arange(N_HEADS, dtype=np.float64))))
    pos = np.arange(C, dtype=np.float64)
    rel = pos[:, None] - pos[None, :]
    decay = np.where(rel >= 0, np.exp(np.maximum(rel, 0.0)[None] * lg[:, None, None]), 0.0)
    lane = np.arange(GROUP_WIDTH)
    lg_qk = lg[(lane % 128) // (RET_QK // 2)]
    lg_v = lg[lane // HEAD_V]
    gq = np.exp((pos + 1.0)[:, None] * lg_qk[None, :])
    gk = np.exp((C - 1.0 - pos)[:, None] * lg_qk[None, :])
    gc = np.exp(C * lg_v)[None, :]
    f = lambda a: jnp.asarray(a.astype(np.float32))
    return f(decay), f(gq), f(gk), f(gc)


def _ret_kernel(z_ref, cos_ref, sin_ref, dec_ref, gq_ref, gk_ref, gc_ref, ng_ref, avg_ref,
                o_ref, st_ref):
    @pl.when(pl.program_id(1) == 0)
    def _():
        st_ref[...] = jnp.zeros_like(st_ref)

    W = GROUP_WIDTH
    z = z_ref[...]
    q1, q2 = z[:, 0:128], z[:, 128:256]
    k1, k2 = z[:, 256:384], z[:, 384:512]
    v = z[:, 2 * W:3 * W]
    g = z[:, 3 * W:]
    c, s = cos_ref[...], sin_ref[...]
    qr = jnp.concatenate([q1 * c - q2 * s, q1 * s + q2 * c], axis=1)
    kr = jnp.concatenate([k1 * c - k2 * s, k1 * s + k2 * c], axis=1) * (RET_QK ** -0.5)
    lane = lax.broadcasted_iota(jnp.int32, (1, W), 1)
    head_qk = (lane & 127) // (RET_QK // 2)
    head_v = lane // HEAD_V
    krb = kr.astype(BF16)
    vb = v.astype(BF16)

    st = st_ref[...]
    o = _dot((qr * gq_ref[...]).astype(BF16), st.astype(BF16))
    for h in range(N_HEADS):
        qm = jnp.where(head_qk == h, qr, 0.0).astype(BF16)
        sc = lax.dot_general(qm, krb, NT_DIMS, preferred_element_type=F32) * dec_ref[h]
        o = o + _dot(sc.astype(BF16), jnp.where(head_v == h, vb, jnp.zeros_like(vb)))

    upd = lax.dot_general((kr * gk_ref[...]).astype(BF16), vb, TN_DIMS, preferred_element_type=F32)
    row_head = (lax.broadcasted_iota(jnp.int32, (W, 1), 0) & 127) // (RET_QK // 2)
    st_ref[...] = st * gc_ref[...] + jnp.where(row_head == head_v, upd, 0.0)

    oc = o - _sum_lanes_f32(o, avg_ref)
    var = _sum_lanes_f32(oc * oc, avg_ref)
    o_ref[...] = (oc * lax.rsqrt(var + EPS) * ng_ref[...] * _silu(g)).astype(o_ref.dtype)


def _ret_call(zr, cos, sin, consts, ng, avg, B, S):
    C = RET_CHUNK
    nc = S // C
    ins = (*consts, ng, avg)
    tok = lambda w: pl.BlockSpec((C, w), lambda b, i: (b * nc + i, 0))
    return pl.pallas_call(
        _ret_kernel,
        grid=(B, nc),
        in_specs=[tok(ZR_W), tok(128), tok(128)] + [_full(a.shape) for a in ins],
        out_specs=tok(GROUP_WIDTH),
        out_shape=jax.ShapeDtypeStruct((B * S, GROUP_WIDTH), BF16),
        scratch_shapes=[pltpu.VMEM((GROUP_WIDTH, GROUP_WIDTH), F32)],
        compiler_params=_params("parallel", "arbitrary"),
        name="retention",
    )(zr, cos, sin, *ins)


def _prep_w_in(w):
    cuts = np.cumsum([0, 256, 128, 32, 256, 256, 512, 512, 256, 256, 256, 256, 256, 256])
    c_q, c_kv, k_pe, z_u, z_v, hq, hf, hi, hg, rq, rk, rv, rg = (
        w[:, cuts[n]:cuts[n + 1]] for n in range(13))
    zeros = lambda n: jnp.zeros((w.shape[0], n), w.dtype)
    half = MLA_ROPE // 2
    mla = [c_q, c_kv, zeros(MLA_NOPE), k_pe[:, :half], k_pe[:, half:], zeros(MLA_HEAD_PAD - MLA_NOPE - MLA_ROPE)]

    def split_halves(m):
        return m.reshape(-1, N_HEADS, 2, RET_QK // 2).transpose(0, 2, 1, 3).reshape(-1, N_HEADS * RET_QK)

    cols = mla + [z_u, z_v, hq, hf, hi, hg, split_halves(rq), split_halves(rk), rv, rg]
    return jnp.concatenate(cols, axis=1).astype(BF16)


def _prep_w_uq(w):
    w = w.reshape(MLA_Q_LORA, N_HEADS, MLA_NOPE + MLA_ROPE)
    w = jnp.pad(w, ((0, 0), (0, 0), (0, MLA_HEAD_PAD - MLA_NOPE - MLA_ROPE)))
    return w.reshape(MLA_Q_LORA, N_HEADS * MLA_HEAD_PAD).astype(BF16)


def _prep_w_ukv(w):
    w = w.reshape(MLA_KV_LORA, N_HEADS, MLA_NOPE + HEAD_V)
    pad = lambda m: jnp.pad(m, ((0, 0), (0, 0), (0, MLA_HEAD_PAD - m.shape[-1]))).reshape(MLA_KV_LORA, -1)
    return jnp.concatenate([pad(w[..., :MLA_NOPE]), pad(w[..., MLA_NOPE:])], axis=1).astype(BF16)


def _rope_tables(positions):
    pos = positions.astype(F32).reshape(-1)[:, None]

    def cos_sin(d):
        inv = ROPE_THETA ** (-jnp.arange(0, d, 2, dtype=F32) / d)
        ang = pos * inv[None, :]
        return jnp.cos(ang), jnp.sin(ang)

    c, s = cos_sin(MLA_ROPE)
    T = pos.shape[0]
    z = lambda n: jnp.zeros((T, n), F32)
    tail = MLA_HEAD_PAD - MLA_NOPE - MLA_ROPE
    cb = jnp.concatenate([jnp.ones((T, MLA_NOPE), F32), c, c, z(tail)], axis=1)
    su = jnp.concatenate([z(MLA_NOPE), z(MLA_ROPE // 2), s, z(tail)], axis=1)
    sd = jnp.concatenate([z(MLA_NOPE), -s, z(MLA_ROPE // 2), z(tail)], axis=1)
    cr, sr = cos_sin(RET_QK)
    return cb, su, sd, jnp.tile(cr, (1, N_HEADS)), jnp.tile(sr, (1, N_HEADS))


def kernel(x, p, positions, ffn1_norm, ffn1_w_gate, ffn1_w_up, ffn1_w_down, mix_norm, w_in, mla_q_norm, mla_w_uq, mla_kv_norm, mla_w_ukv, sg_ln, sg_w_s, sg_b_s, hg_lb_logits, hg_norm, ret_norm, w_out, ffn2_norm, ffn2_w_gate, ffn2_w_up, ffn2_w_down, ple_norm, ple_w_proj, ple_w_gate, final_norm):
    B, S, D = x.shape
    L = w_in.shape[0]
    T = B * S
    row = lambda a: a.reshape(1, -1).astype(F32)

    cb, su, sd, cos_r, sin_r = _rope_tables(positions)
    hg_consts = _hg_constants()
    ret_consts = _ret_constants()
    avg = _head_avg3()
    lb_all = jnp.cumsum(jax.nn.softmax(hg_lb_logits.astype(F32), axis=0), axis=0)
    lb_all = lb_all - lb_all[0:1]
    gf = row(final_norm)

    h = x.reshape(T, D)
    for l in range(L):
        bf = lambda a: a[l].astype(BF16)
        h, zm, zs, zh, zr = _pre_call(h, row(ffn1_norm[l]), bf(ffn1_w_gate), bf(ffn1_w_up), bf(ffn1_w_down),
                                      row(mix_norm[l]), _prep_w_in(w_in[l]))

        q, k, v = _mla_prep_call(zm, row(mla_q_norm[l]), row(mla_kv_norm[l]), _prep_w_uq(mla_w_uq[l]),
                                 _prep_w_ukv(mla_w_ukv[l]), cb, su, sd)
        y_a = _attn_call(q, k, v, B, S)
        sg_bias = jnp.repeat(sg_b_s[l].T, HEAD_V, axis=1)
        y_b = _sg_call(zs, row(sg_ln[l]), sg_w_s[l], sg_bias)
        y_c = _hg_call(zh, row(lb_all[l]), row(hg_norm[l]), hg_consts, avg, B, S)
        y_d = _ret_call(zr, cos_r, sin_r, ret_consts, row(ret_norm[l]), avg, B, S)

        h = _post_call(h, (y_a, y_b, y_c, y_d), p[l].reshape(T, PLE_DIM), bf(w_out),
                       row(ffn2_norm[l]), bf(ffn2_w_gate), bf(ffn2_w_up), bf(ffn2_w_down),
                       row(ple_norm[l]), bf(ple_w_proj), bf(ple_w_gate), gf, final=(l == L - 1))
    return h.reshape(B, S, D)
```

```python
import functools

import numpy as np
import jax
import jax.numpy as jnp
from jax import lax
from jax.experimental import pallas as pl
from jax.experimental.pallas import tpu as pltpu

F32 = jnp.float32
BF16 = jnp.bfloat16
EPS = 1e-6
ROPE_THETA = 10000.0

D_MODEL = 1024
D_FF = 2816
FF_CHUNK = 256
N_FF_CHUNKS = D_FF // FF_CHUNK
PLE_DIM = 256
GROUP_WIDTH = 256
N_HEADS = 4
HEAD_V = GROUP_WIDTH // N_HEADS

MLA_NOPE, MLA_ROPE, MLA_Q_LORA, MLA_KV_LORA = 64, 32, 256, 128
MLA_HEAD_PAD = 128
MLA_W = N_HEADS * MLA_HEAD_PAD
SG_CHUNK = 128
HG_KEY = 128
HG_KW = N_HEADS * HG_KEY
HG_SUB = 8
HG_GROUP = 64
HG_TILE = 256
RET_QK = 64
RET_CHUNK = 256

ZM_W, ZS_W, ZH_W, ZR_W = 512, 512, 1536, 1024

TOKEN_TILE = 512
ATTN_TILE = 512
VMEM_LIMIT = 60 * 1024 * 1024

NT_DIMS = (((1,), (1,)), ((), ()))
TN_DIMS = (((0,), (0,)), ((), ()))


def _rms(x, g):
    return x * lax.rsqrt(jnp.mean(x * x, axis=-1, keepdims=True) + EPS) * g


def _silu(x):
    return x * jax.nn.sigmoid(x)


def _dot(a, b):
    return jnp.dot(a, b, preferred_element_type=F32)


def _params(*sem):
    return pltpu.CompilerParams(dimension_semantics=sem, vmem_limit_bytes=VMEM_LIMIT)


def _full(shape):
    return pl.BlockSpec(shape, lambda *_: (0,) * len(shape))


def _split3(x):
    p1 = x.astype(BF16)
    r = x - p1.astype(F32)
    p2 = r.astype(BF16)
    p3 = (r - p2.astype(F32)).astype(BF16)
    return p1, p2, p3


def _sum_rows_f32(m3_ref, x):
    return _dot(m3_ref[...], jnp.concatenate(_split3(x), axis=0))


def _sum_lanes_f32(x, m3_ref):
    return _dot(jnp.concatenate(_split3(x), axis=1), m3_ref[...])


def _ffn_half_step(h, g_ref, wg_ref, wu_ref, wd_ref):
    x = _rms(h, g_ref[...]).astype(BF16)
    acc = jnp.zeros_like(h)
    for c in range(N_FF_CHUNKS):
        cols = slice(c * FF_CHUNK, (c + 1) * FF_CHUNK)
        a = (_silu(_dot(x, wg_ref[:, cols])) * _dot(x, wu_ref[:, cols])).astype(BF16)
        acc = acc + _dot(a, wd_ref[cols, :])
    return h + 0.5 * acc


def _rope_lanes(x, cb, su, sd):
    w = x.shape[1]
    half = MLA_ROPE // 2
    return x * cb + pltpu.roll(x, half, 1) * su + pltpu.roll(x, w - half, 1) * sd


def _mla_prep(z, qn_ref, kvn_ref, wuq_ref, wukv_ref, cb, su, sd):
    c_q = z[:, :MLA_Q_LORA]
    c_kv = z[:, MLA_Q_LORA:MLA_Q_LORA + MLA_KV_LORA]
    k_pe = z[:, MLA_Q_LORA + MLA_KV_LORA:]
    q = _dot(_rms(c_q, qn_ref[...]).astype(BF16), wuq_ref[...])
    kv = _dot(_rms(c_kv, kvn_ref[...]).astype(BF16), wukv_ref[...])
    tile4 = lambda t: jnp.concatenate([t] * N_HEADS, axis=1)
    scale = (MLA_NOPE + MLA_ROPE) ** -0.5
    q = (_rope_lanes(q, tile4(cb), tile4(su), tile4(sd)) * scale).astype(BF16)
    k = (kv[:, :MLA_W] + tile4(_rope_lanes(k_pe, cb, su, sd))).astype(BF16)
    return q, k, kv[:, MLA_W:].astype(BF16)


def _attn_kernel(q_ref, k_ref, v_ref, o_ref):
    i = pl.program_id(1)
    tq = tk = ATTN_TILE
    row = lax.broadcasted_iota(jnp.int32, (tq, tk), 0)
    col = lax.broadcasted_iota(jnp.int32, (tq, tk), 1)
    causal = row >= col
    head_lanes = [slice(h * MLA_HEAD_PAD, (h + 1) * MLA_HEAD_PAD) for h in range(N_HEADS)]

    def step(j, carry, masked):
        start = pl.multiple_of(j * tk, tk)
        out = []
        for lanes, (m, l, acc) in zip(head_lanes, carry):
            kb = k_ref[pl.ds(start, tk), lanes]
            vb = v_ref[pl.ds(start, tk), lanes]
            s = lax.dot_general(q_ref[:, lanes], kb, NT_DIMS, preferred_element_type=F32)
            if masked:
                s = jnp.where(causal, s, -jnp.inf)
            m_new = jnp.maximum(m, jnp.max(s, axis=1, keepdims=True))
            p = jnp.exp(s - m_new)
            alpha = jnp.exp(m - m_new)
            l = alpha * l + jnp.sum(p, axis=1, keepdims=True)
            acc = alpha * acc + _dot(p.astype(BF16), vb)
            out.append((m_new, l, acc))
        return tuple(out)

    init = tuple((jnp.full((tq, 1), -jnp.inf, F32), jnp.zeros((tq, 1), F32),
                  jnp.zeros((tq, MLA_HEAD_PAD), F32)) for _ in range(N_HEADS))
    carry = lax.fori_loop(0, i, functools.partial(step, masked=False), init)
    carry = step(i, carry, True)
    heads = [acc / l for _, l, acc in carry]
    lane = lax.broadcasted_iota(jnp.int32, (tq, MLA_HEAD_PAD), 1)
    lo = lane < HEAD_V
    pair = lambda a, b: jnp.where(lo, a, pltpu.roll(b, HEAD_V, 1))
    o_ref[...] = jnp.concatenate([pair(heads[0], heads[1]), pair(heads[2], heads[3])],
                                 axis=1).astype(o_ref.dtype)


def _attn_call(q, k, v, B, S):
    nq = S // ATTN_TILE
    return pl.pallas_call(
        _attn_kernel,
        grid=(B, nq),
        in_specs=[pl.BlockSpec((ATTN_TILE, MLA_W), lambda b, i: (b * nq + i, 0)),
                  pl.BlockSpec((S, MLA_W), lambda b, i: (b, 0)),
                  pl.BlockSpec((S, MLA_W), lambda b, i: (b, 0))],
        out_specs=pl.BlockSpec((ATTN_TILE, GROUP_WIDTH), lambda b, i: (b * nq + i, 0)),
        out_shape=jax.ShapeDtypeStruct((B * S, GROUP_WIDTH), BF16),
        compiler_params=_params("parallel", "arbitrary"),
        name="mla_attention",
    )(q, k, v)


def _gelu(x):
    return 0.5 * x * (1.0 + lax.erf(x * np.float32(np.sqrt(0.5))))


def _spatial_gating(z, ln_ref, w_ref, bias_ref):
    u = _gelu(z[:, :GROUP_WIDTH])
    gv = _gelu(z[:, GROUP_WIDTH:])
    xc = gv - jnp.mean(gv, axis=-1, keepdims=True)
    v = xc * lax.rsqrt(jnp.mean(xc * xc, axis=-1, keepdims=True) + EPS) * ln_ref[...]
    C = SG_CHUNK
    causal = (lax.broadcasted_iota(jnp.int32, (C, C), 0) >= lax.broadcasted_iota(jnp.int32, (C, C), 1))
    head = lax.broadcasted_iota(jnp.int32, (1, GROUP_WIDTH), 1) // HEAD_V
    ws = [jnp.where(causal, w_ref[h], 0.0).astype(BF16) for h in range(N_HEADS)]
    bias = bias_ref[...]
    out = []
    for c in range(z.shape[0] // C):
        rows = slice(c * C, (c + 1) * C)
        vc = v[rows]
        mixed = bias
        for h in range(N_HEADS):
            mixed = mixed + _dot(ws[h], jnp.where(head == h, vc, 0.0).astype(BF16))
        out.append((u[rows] * mixed).astype(BF16))
    return jnp.concatenate(out, axis=0)


def _hg_constants():
    t = np.arange(HG_TILE)
    same = (t[:, None] // HG_GROUP) == (t[None, :] // HG_GROUP)
    m_cum = (same & (t[None, :] <= t[:, None])).astype(np.float32)
    kg = np.arange(HG_KW)
    e = np.arange(GROUP_WIDTH)
    ind = (kg[:, None] // HG_KEY == e[None, :] // HEAD_V).astype(np.float32)
    return jnp.asarray(np.tile(m_cum, (1, 3)), dtype=BF16), jnp.asarray(ind, dtype=BF16)


def _head_avg3():
    e = np.arange(GROUP_WIDTH)
    avg = (e[:, None] // HEAD_V == e[None, :] // HEAD_V).astype(np.float32) / HEAD_V
    return jnp.asarray(np.tile(avg, (3, 1)), dtype=BF16)


def _hgrn2(z, lb_ref, ng_ref, mcum3_ref, ind_ref, avg3_ref, st):
    TT, KW = HG_TILE, HG_KW
    n_grp, n_sub = TT // HG_GROUP, HG_GROUP // HG_SUB
    q = z[:, :KW]
    f = z[:, KW:2 * KW]
    vi = z[:, 2 * KW:2 * KW + GROUP_WIDTH]
    g = z[:, 2 * KW + GROUP_WIDTH:]
    lb = lb_ref[...]
    forget = lb + (1.0 - lb) * jax.nn.sigmoid(f)
    kk = 1.0 - forget
    b = _sum_rows_f32(mcum3_ref, jnp.log(forget))
    head_v = lax.broadcasted_iota(jnp.int32, (1, GROUP_WIDTH), 1) // HEAD_V
    ind = ind_ref[...]

    tiles = lambda a: a.reshape(TT // HG_SUB, HG_SUB, a.shape[-1])
    q3, f3, v3 = tiles(q), tiles(forget), tiles(vi)
    r_sub = lax.broadcasted_iota(jnp.int32, (1, HG_SUB, 1), 1)
    o = _dot((q * kk).astype(BF16), ind) * vi
    decay, f_prev = None, f3
    for d in range(1, HG_SUB):
        f_d = pltpu.roll(f3, d, 1)
        decay = f3 if d == 1 else decay * f_prev
        x = jnp.where(r_sub >= d, q3 * (1.0 - f_d) * decay, 0.0)
        w = _dot(x.reshape(TT, KW).astype(BF16), ind)
        o = o + w * pltpu.roll(v3, d, 1).reshape(TT, GROUP_WIDTH)
        f_prev = f_d

    groups = lambda a: a.reshape(n_grp, n_sub, HG_SUB, a.shape[-1])
    b4, kk4, q4 = groups(b), groups(kk), groups(q)
    ends = jnp.broadcast_to(b4[:, :, HG_SUB - 1:HG_SUB, :], b4.shape)
    bsub4 = jnp.concatenate([jnp.zeros_like(ends[:, :1]), ends[:, :n_sub - 1]], axis=1)
    blast4 = ends[:, n_sub - 1:]

    qp4 = q4 * jnp.exp(b4 - bsub4)
    zeros4 = lambda n: [jnp.zeros((n_grp, n, HG_SUB, KW), F32)] if n else []
    q_slots, k_slots = [], []
    for i in range(1, n_sub):
        k_i = kk4[:, :i] * jnp.exp(bsub4[:, i:i + 1] - b4[:, :i])
        k_slots.append(jnp.concatenate([k_i] + zeros4(n_sub - i), axis=1).reshape(TT, KW))
        q_slots.append(jnp.concatenate(zeros4(i) + [qp4[:, i:i + 1]] + zeros4(n_sub - 1 - i),
                                       axis=1).reshape(TT, KW))
    same_group = ((lax.broadcasted_iota(jnp.int32, (TT, TT), 0) // HG_GROUP)
                  == (lax.broadcasted_iota(jnp.int32, (TT, TT), 1) // HG_GROUP))
    for h in range(N_HEADS):
        lanes = slice(h * HG_KEY, (h + 1) * HG_KEY)
        lhs = jnp.concatenate([s[:, lanes] for s in q_slots], axis=1).astype(BF16)
        rhs = jnp.concatenate([s[:, lanes] for s in k_slots], axis=1).astype(BF16)
        a = lax.dot_general(lhs, rhs, NT_DIMS, preferred_element_type=F32)
        a = jnp.where(same_group, a, 0.0).astype(BF16)
        o = o + _dot(a, jnp.where(head_v == h, vi, 0.0).astype(BF16))

    qe = (q * jnp.exp(b)).astype(BF16)
    k_dec = (kk4 * jnp.exp(blast4 - b4)).reshape(TT, KW).astype(BF16)
    e_last = jnp.exp(blast4[:, 0, 0:1, :])
    vb = vi.astype(BF16)
    same_head = (lax.broadcasted_iota(jnp.int32, (GROUP_WIDTH, 1), 0) // HEAD_V
                 == lax.broadcasted_iota(jnp.int32, (1, KW), 1) // HG_KEY)
    inter = []
    for gi in range(n_grp):
        rows = slice(gi * HG_GROUP, (gi + 1) * HG_GROUP)
        inter.append(lax.dot_general(qe[rows], st.astype(BF16), NT_DIMS, preferred_element_type=F32))
        upd = lax.dot_general(vb[rows], k_dec[rows], TN_DIMS, preferred_element_type=F32)
        st = st * e_last[gi] + jnp.where(same_head, upd, 0.0)
    o = o + jnp.concatenate(inter, axis=0)

    ms = _sum_lanes_f32(o * o, avg3_ref)
    return (o * lax.rsqrt(ms + EPS) * ng_ref[...] * _silu(g)).astype(BF16), st


def _ret_constants():
    C = RET_CHUNK
    lg = np.log1p(-np.power(2.0, -(5.0 + np.arange(N_HEADS, dtype=np.float64))))
    pos = np.arange(C, dtype=np.float64)
    rel = pos[:, None] - pos[None, :]
    decay = np.where(rel >= 0, np.exp(np.maximum(rel, 0.0)[None] * lg[:, None, None]), 0.0)
    lane = np.arange(GROUP_WIDTH)
    lg_qk = lg[(lane % 128) // (RET_QK // 2)]
    lg_v = lg[lane // HEAD_V]
    gq = np.exp((pos + 1.0)[:, None] * lg_qk[None, :])
    gk = np.exp((C - 1.0 - pos)[:, None] * lg_qk[None, :])
    gc = np.exp(C * lg_v)[None, :]
    f = lambda a: jnp.asarray(a.astype(np.float32))
    return f(decay), f(gq), f(gk), f(gc)


def _retention(z, c, s, dec_ref, gq_ref, gk_ref, gc_ref, ng_ref, avg3_ref, st):
    W = GROUP_WIDTH
    q1, q2 = z[:, 0:128], z[:, 128:256]
    k1, k2 = z[:, 256:384], z[:, 384:512]
    v = z[:, 2 * W:3 * W]
    g = z[:, 3 * W:]
    qr = jnp.concatenate([q1 * c - q2 * s, q1 * s + q2 * c], axis=1)
    kr = jnp.concatenate([k1 * c - k2 * s, k1 * s + k2 * c], axis=1) * (RET_QK ** -0.5)
    lane = lax.broadcasted_iota(jnp.int32, (1, W), 1)
    head_qk = (lane & 127) // (RET_QK // 2)
    head_v = lane // HEAD_V
    krb = kr.astype(BF16)
    vb = v.astype(BF16)

    o = _dot((qr * gq_ref[...]).astype(BF16), st.astype(BF16))
    for h in range(N_HEADS):
        qm = jnp.where(head_qk == h, qr, 0.0).astype(BF16)
        sc = lax.dot_general(qm, krb, NT_DIMS, preferred_element_type=F32) * dec_ref[h]
        o = o + _dot(sc.astype(BF16), jnp.where(head_v == h, vb, jnp.zeros_like(vb)))

    upd = lax.dot_general((kr * gk_ref[...]).astype(BF16), vb, TN_DIMS, preferred_element_type=F32)
    row_head = (lax.broadcasted_iota(jnp.int32, (W, 1), 0) & 127) // (RET_QK // 2)
    st = st * gc_ref[...] + jnp.where(row_head == head_v, upd, 0.0)

    oc = o - _sum_lanes_f32(o, avg3_ref)
    var = _sum_lanes_f32(oc * oc, avg3_ref)
    return (oc * lax.rsqrt(var + EPS) * ng_ref[...] * _silu(g)).astype(BF16), st


def _mix_kernel(tiles_per_seq,
                h_ref, cb_ref, su_ref, sd_ref, cos_ref, sin_ref,
                g1_ref, wg_ref, wu_ref, wd_ref, g2_ref, win_ref,
                qn_ref, kvn_ref, wuq_ref, wukv_ref,
                sgln_ref, sgw_ref, sgb_ref,
                lb_ref, hgn_ref, mcum3_ref, ind_ref, avg3_ref,
                dec_ref, gq_ref, gk_ref, gc_ref, retn_ref,
                h_out, q_out, k_out, v_out, yb_out, yc_out, yd_out,
                hg_st, ret_st):
    @pl.when(pl.program_id(0) % tiles_per_seq == 0)
    def _():
        hg_st[...] = jnp.zeros_like(hg_st)
        ret_st[...] = jnp.zeros_like(ret_st)

    h1 = _ffn_half_step(h_ref[...], g1_ref, wg_ref, wu_ref, wd_ref)
    h_out[...] = h1
    x = _rms(h1, g2_ref[...]).astype(BF16)
    zs = []
    off = 0
    for w in (ZM_W, ZS_W, ZH_W, ZR_W):
        zs.append(_dot(x, win_ref[:, off:off + w]))
        off += w
    zm, zg, zh, zr = zs

    q, k, v = _mla_prep(zm, qn_ref, kvn_ref, wuq_ref, wukv_ref, cb_ref[...], su_ref[...], sd_ref[...])
    q_out[...] = q
    k_out[...] = k
    v_out[...] = v
    yb_out[...] = _spatial_gating(zg, sgln_ref, sgw_ref, sgb_ref)

    st = hg_st[...]
    for n in range(TOKEN_TILE // HG_TILE):
        rows = slice(n * HG_TILE, (n + 1) * HG_TILE)
        y, st = _hgrn2(zh[rows], lb_ref, hgn_ref, mcum3_ref, ind_ref, avg3_ref, st)
        yc_out[rows, :] = y
    hg_st[...] = st

    st = ret_st[...]
    cos, sin = cos_ref[...], sin_ref[...]
    for n in range(TOKEN_TILE // RET_CHUNK):
        rows = slice(n * RET_CHUNK, (n + 1) * RET_CHUNK)
        y, st = _retention(zr[rows], cos[rows], sin[rows], dec_ref, gq_ref, gk_ref, gc_ref, retn_ref,
                           avg3_ref, st)
        yd_out[rows, :] = y
    ret_st[...] = st


def _mix_call(h, tables, weights, S):
    T = h.shape[0]
    tm = TOKEN_TILE
    tok = lambda w: pl.BlockSpec((tm, w), lambda i: (i, 0))
    out_w = (D_MODEL, MLA_W, MLA_W, MLA_W, GROUP_WIDTH, GROUP_WIDTH, GROUP_WIDTH)
    out_t = (F32,) + (BF16,) * 6
    return pl.pallas_call(
        functools.partial(_mix_kernel, S // tm),
        grid=(T // tm,),
        in_specs=[tok(D_MODEL)] + [tok(128)] * 5 + [_full(a.shape) for a in weights],
        out_specs=[tok(w) for w in out_w],
        out_shape=[jax.ShapeDtypeStruct((T, w), t) for w, t in zip(out_w, out_t)],
        scratch_shapes=[pltpu.VMEM((GROUP_WIDTH, HG_KW), F32), pltpu.VMEM((GROUP_WIDTH, GROUP_WIDTH), F32)],
        compiler_params=_params("arbitrary"),
        name="ffn_inproj_mixers",
    )(h, *tables, *weights)


def _post_kernel(final, h_ref, ya_ref, yb_ref, yc_ref, yd_ref, p_ref, wo_ref, g_ref, wg_ref, wu_ref,
                 wd_ref, gp_ref, wpp_ref, wpg_ref, gf_ref, out_ref):
    h = h_ref[...]
    for n, y_ref in enumerate((ya_ref, yb_ref, yc_ref, yd_ref)):
        h = h + _dot(y_ref[...], wo_ref[n * GROUP_WIDTH:(n + 1) * GROUP_WIDTH, :])
    h = _ffn_half_step(h, g_ref, wg_ref, wu_ref, wd_ref)
    gate = jax.nn.sigmoid(_dot(_rms(h, gp_ref[...]).astype(BF16), wpg_ref[...]))
    h = h + _dot(p_ref[...].astype(BF16), wpp_ref[...]) * gate
    if final:
        h = _rms(h, gf_ref[...])
    out_ref[...] = h


def _post_call(h, ys, p, wo, g, wg, wu, wd, gp, wpp, wpg, gf, final):
    T = h.shape[0]
    tm = TOKEN_TILE
    tok = lambda w: pl.BlockSpec((tm, w), lambda i: (i, 0))
    return pl.pallas_call(
        functools.partial(_post_kernel, final),
        grid=(T // tm,),
        in_specs=[tok(D_MODEL)] + [tok(GROUP_WIDTH)] * 4 + [tok(PLE_DIM)]
        + [_full(a.shape) for a in (wo, g, wg, wu, wd, gp, wpp, wpg, gf)],
        out_specs=tok(D_MODEL),
        out_shape=jax.ShapeDtypeStruct((T, D_MODEL), F32),
        compiler_params=_params("parallel"),
        name="post_outproj_ffn_ple",
    )(h, *ys, p, wo, g, wg, wu, wd, gp, wpp, wpg, gf)


def _prep_w_in(w):
    cuts = np.cumsum([0, 256, 128, 32, 256, 256, 512, 512, 256, 256, 256, 256, 256, 256])
    c_q, c_kv, k_pe, z_u, z_v, hq, hf, hi, hg, rq, rk, rv, rg = (
        w[:, cuts[n]:cuts[n + 1]] for n in range(13))
    zeros = lambda n: jnp.zeros((w.shape[0], n), w.dtype)
    half = MLA_ROPE // 2
    mla = [c_q, c_kv, zeros(MLA_NOPE), k_pe[:, :half], k_pe[:, half:], zeros(MLA_HEAD_PAD - MLA_NOPE - MLA_ROPE)]

    def split_halves(m):
        return m.reshape(-1, N_HEADS, 2, RET_QK // 2).transpose(0, 2, 1, 3).reshape(-1, N_HEADS * RET_QK)

    cols = mla + [z_u, z_v, hq, hf, hi, hg, split_halves(rq), split_halves(rk), rv, rg]
    return jnp.concatenate(cols, axis=1).astype(BF16)


def _prep_w_uq(w):
    w = w.reshape(MLA_Q_LORA, N_HEADS, MLA_NOPE + MLA_ROPE)
    w = jnp.pad(w, ((0, 0), (0, 0), (0, MLA_HEAD_PAD - MLA_NOPE - MLA_ROPE)))
    return w.reshape(MLA_Q_LORA, MLA_W).astype(BF16)


def _prep_w_ukv(w):
    w = w.reshape(MLA_KV_LORA, N_HEADS, MLA_NOPE + HEAD_V)
    pad = lambda m: jnp.pad(m, ((0, 0), (0, 0), (0, MLA_HEAD_PAD - m.shape[-1]))).reshape(MLA_KV_LORA, -1)
    return jnp.concatenate([pad(w[..., :MLA_NOPE]), pad(w[..., MLA_NOPE:])], axis=1).astype(BF16)


def _rope_tables(positions):
    pos = positions.astype(F32).reshape(-1)[:, None]

    def cos_sin(d):
        inv = ROPE_THETA ** (-jnp.arange(0, d, 2, dtype=F32) / d)
        ang = pos * inv[None, :]
        return jnp.cos(ang), jnp.sin(ang)

    c, s = cos_sin(MLA_ROPE)
    T = pos.shape[0]
    z = lambda n: jnp.zeros((T, n), F32)
    tail = MLA_HEAD_PAD - MLA_NOPE - MLA_ROPE
    cb = jnp.concatenate([jnp.ones((T, MLA_NOPE), F32), c, c, z(tail)], axis=1)
    su = jnp.concatenate([z(MLA_NOPE), z(MLA_ROPE // 2), s, z(tail)], axis=1)
    sd = jnp.concatenate([z(MLA_NOPE), -s, z(MLA_ROPE // 2), z(tail)], axis=1)
    cr, sr = cos_sin(RET_QK)
    return cb, su, sd, jnp.tile(cr, (1, N_HEADS)), jnp.tile(sr, (1, N_HEADS))


def kernel(x, p, positions, ffn1_norm, ffn1_w_gate, ffn1_w_up, ffn1_w_down, mix_norm, w_in, mla_q_norm, mla_w_uq, mla_kv_norm, mla_w_ukv, sg_ln, sg_w_s, sg_b_s, hg_lb_logits, hg_norm, ret_norm, w_out, ffn2_norm, ffn2_w_gate, ffn2_w_up, ffn2_w_down, ple_norm, ple_w_proj, ple_w_gate, final_norm):
    B, S, D = x.shape
    L = w_in.shape[0]
    T = B * S
    row = lambda a: a.reshape(1, -1).astype(F32)

    tables = _rope_tables(positions)
    mcum3, ind = _hg_constants()
    dec, gq, gk, gc = _ret_constants()
    avg3 = _head_avg3()
    lb_all = jnp.cumsum(jax.nn.softmax(hg_lb_logits.astype(F32), axis=0), axis=0)
    lb_all = lb_all - lb_all[0:1]
    gf = row(final_norm)

    h = x.reshape(T, D)
    for l in range(L):
        bf = lambda a: a[l].astype(BF16)
        sg_bias = jnp.repeat(sg_b_s[l].T, HEAD_V, axis=1)
        weights = (row(ffn1_norm[l]), bf(ffn1_w_gate), bf(ffn1_w_up), bf(ffn1_w_down),
                   row(mix_norm[l]), _prep_w_in(w_in[l]),
                   row(mla_q_norm[l]), row(mla_kv_norm[l]), _prep_w_uq(mla_w_uq[l]), _prep_w_ukv(mla_w_ukv[l]),
                   row(sg_ln[l]), sg_w_s[l], sg_bias,
                   row(lb_all[l]), row(hg_norm[l]), mcum3, ind, avg3,
                   dec, gq, gk, gc, row(ret_norm[l]))
        h, q, k, v, y_b, y_c, y_d = _mix_call(h, tables, weights, S)
        y_a = _attn_call(q, k, v, B, S)
        h = _post_call(h, (y_a, y_b, y_c, y_d), p[l].reshape(T, PLE_DIM), bf(w_out),
                       row(ffn2_norm[l]), bf(ffn2_w_gate), bf(ffn2_w_up), bf(ffn2_w_down),
                       row(ple_norm[l]), bf(ple_w_proj), bf(ple_w_gate), gf, final=(l == L - 1))
    return h.reshape(B, S, D)
```

```python
import functools
from typing import NamedTuple

import numpy as np
import jax
import jax.numpy as jnp
from jax import lax
from jax.experimental import pallas as pl
from jax.experimental.pallas import tpu as pltpu

F32 = jnp.float32
BF16 = jnp.bfloat16
EPS = 1e-6
ROPE_THETA = 10000.0

D_MODEL = 1024
D_FF = 2816
FF_CHUNK = 256
N_FF_CHUNKS = D_FF // FF_CHUNK
PLE_DIM = 256
GROUP_WIDTH = 256
N_HEADS = 4
HEAD_V = GROUP_WIDTH // N_HEADS

MLA_NOPE, MLA_ROPE, MLA_Q_LORA, MLA_KV_LORA = 64, 32, 256, 128
MLA_HEAD_PAD = 128
MLA_W = N_HEADS * MLA_HEAD_PAD
SG_CHUNK = 128
HG_KEY = 128
HG_KW = N_HEADS * HG_KEY
HG_SUB = 8
HG_GROUP = 64
HG_TILE = 256
RET_QK = 64
RET_CHUNK = 256

ZM_W, ZS_W, ZH_W, ZR_W = 512, 512, 1536, 1024

TOKEN_TILE = 512
ATTN_Q_TILE = 1024
ATTN_KV_TILE = 512
VMEM_LIMIT = 60 * 1024 * 1024

NT_DIMS = (((1,), (1,)), ((), ()))
TN_DIMS = (((0,), (0,)), ((), ()))


def _rms(x, g):
    return x * lax.rsqrt(jnp.mean(x * x, axis=-1, keepdims=True) + EPS) * g


def _silu(x):
    return x * jax.nn.sigmoid(x)


def _dot(a, b):
    return jnp.dot(a, b, preferred_element_type=F32)


def _params(*sem):
    return pltpu.CompilerParams(dimension_semantics=sem, vmem_limit_bytes=VMEM_LIMIT)


def _full(shape):
    return pl.BlockSpec(shape, lambda *_: (0,) * len(shape))


class _Layer(NamedTuple):
    stacked: jax.Array
    index: int


def _weight_spec(w):
    if isinstance(w, _Layer):
        shape = w.stacked.shape[1:]
        return pl.BlockSpec((None,) + shape, lambda *_: (w.index,) + (0,) * len(shape),
                            pipeline_mode=pl.Buffered(1))
    return _full(w.shape)


def _weight_arg(w):
    return w.stacked if isinstance(w, _Layer) else w


def _split3(x):
    p1 = x.astype(BF16)
    r = x - p1.astype(F32)
    p2 = r.astype(BF16)
    p3 = (r - p2.astype(F32)).astype(BF16)
    return p1, p2, p3


def _sum_rows_f32(m3_ref, x):
    return _dot(m3_ref[...], jnp.concatenate(_split3(x), axis=0))


def _sum_lanes_f32(x, m3_ref):
    return _dot(jnp.concatenate(_split3(x), axis=1), m3_ref[...])


def _ffn_half_step(h, g_ref, wg_ref, wu_ref, wd_ref):
    x = _rms(h, g_ref[...]).astype(BF16)
    acc = jnp.zeros_like(h)
    for c in range(N_FF_CHUNKS):
        cols = slice(c * FF_CHUNK, (c + 1) * FF_CHUNK)
        a = (_silu(_dot(x, wg_ref[:, cols])) * _dot(x, wu_ref[:, cols])).astype(BF16)
        acc = acc + _dot(a, wd_ref[cols, :])
    return h + 0.5 * acc


def _rope_lanes(x, cb, su, sd):
    w = x.shape[1]
    half = MLA_ROPE // 2
    return x * cb + pltpu.roll(x, half, 1) * su + pltpu.roll(x, w - half, 1) * sd


def _mla_prep(z, qn_ref, kvn_ref, wuq_ref, wukv_ref, cb, su, sd):
    c_q = z[:, :MLA_Q_LORA]
    c_kv = z[:, MLA_Q_LORA:MLA_Q_LORA + MLA_KV_LORA]
    k_pe = z[:, MLA_Q_LORA + MLA_KV_LORA:]
    q = _dot(_rms(c_q, qn_ref[...]).astype(BF16), wuq_ref[...])
    kv = _dot(_rms(c_kv, kvn_ref[...]).astype(BF16), wukv_ref[...])
    tile4 = lambda t: jnp.concatenate([t] * N_HEADS, axis=1)
    scale = (MLA_NOPE + MLA_ROPE) ** -0.5
    q = (_rope_lanes(q, tile4(cb), tile4(su), tile4(sd)) * scale).astype(BF16)
    k = (kv[:, :MLA_W] + tile4(_rope_lanes(k_pe, cb, su, sd))).astype(BF16)
    return q, k, kv[:, MLA_W:].astype(BF16)


def _attn_kernel(q_ref, k_ref, v_ref, o_ref):
    i = pl.program_id(1)
    tq, tk = ATTN_Q_TILE, ATTN_KV_TILE
    row = lax.broadcasted_iota(jnp.int32, (tq, tk), 0)
    col = lax.broadcasted_iota(jnp.int32, (tq, tk), 1)
    head_lanes = [slice(h * MLA_HEAD_PAD, (h + 1) * MLA_HEAD_PAD) for h in range(N_HEADS)]

    def step(j, carry, visible):
        start = pl.multiple_of(j * tk, tk)
        out = []
        for lanes, (m, l, acc) in zip(head_lanes, carry):
            kb = k_ref[pl.ds(start, tk), lanes]
            vb = v_ref[pl.ds(start, tk), lanes]
            s = lax.dot_general(q_ref[:, lanes], kb, NT_DIMS, preferred_element_type=F32)
            if visible is not None:
                s = jnp.where(visible, s, -jnp.inf)
            m_new = jnp.maximum(m, jnp.max(s, axis=1, keepdims=True))
            p = jnp.exp(s - m_new)
            alpha = jnp.exp(m - m_new)
            l = alpha * l + jnp.sum(p, axis=1, keepdims=True)
            acc = alpha * acc + _dot(p.astype(BF16), vb)
            out.append((m_new, l, acc))
        return tuple(out)

    init = tuple((jnp.full((tq, 1), -jnp.inf, F32), jnp.zeros((tq, 1), F32),
                  jnp.zeros((tq, MLA_HEAD_PAD), F32)) for _ in range(N_HEADS))
    per_q = tq // tk
    carry = lax.fori_loop(0, i * per_q, functools.partial(step, visible=None), init)
    for d in range(per_q):
        carry = step(i * per_q + d, carry, row >= col + d * tk)
    heads = [acc / l for _, l, acc in carry]
    lane = lax.broadcasted_iota(jnp.int32, (tq, MLA_HEAD_PAD), 1)
    lo = lane < HEAD_V
    pair = lambda a, b: jnp.where(lo, a, pltpu.roll(b, HEAD_V, 1))
    o_ref[...] = jnp.concatenate([pair(heads[0], heads[1]), pair(heads[2], heads[3])],
                                 axis=1).astype(o_ref.dtype)


def _attn_call(q, k, v, B, S):
    nq = S // ATTN_Q_TILE
    return pl.pallas_call(
        _attn_kernel,
        grid=(B, nq),
        in_specs=[pl.BlockSpec((ATTN_Q_TILE, MLA_W), lambda b, i: (b * nq + i, 0)),
                  pl.BlockSpec((S, MLA_W), lambda b, i: (b, 0)),
                  pl.BlockSpec((S, MLA_W), lambda b, i: (b, 0))],
        out_specs=pl.BlockSpec((ATTN_Q_TILE, GROUP_WIDTH), lambda b, i: (b * nq + i, 0)),
        out_shape=jax.ShapeDtypeStruct((B * S, GROUP_WIDTH), BF16),
        compiler_params=_params("parallel", "arbitrary"),
        name="mla_attention",
    )(q, k, v)


def _gelu(x):
    return 0.5 * x * (1.0 + lax.erf(x * np.float32(np.sqrt(0.5))))


def _spatial_gating(z, ln_ref, w_ref, bias_ref):
    u = _gelu(z[:, :GROUP_WIDTH])
    gv = _gelu(z[:, GROUP_WIDTH:])
    xc = gv - jnp.mean(gv, axis=-1, keepdims=True)
    v = xc * lax.rsqrt(jnp.mean(xc * xc, axis=-1, keepdims=True) + EPS) * ln_ref[...]
    C = SG_CHUNK
    causal = (lax.broadcasted_iota(jnp.int32, (C, C), 0) >= lax.broadcasted_iota(jnp.int32, (C, C), 1))
    head = lax.broadcasted_iota(jnp.int32, (1, GROUP_WIDTH), 1) // HEAD_V
    ws = [jnp.where(causal, w_ref[h], 0.0).astype(BF16) for h in range(N_HEADS)]
    bias = bias_ref[...]
    out = []
    for c in range(z.shape[0] // C):
        rows = slice(c * C, (c + 1) * C)
        vc = v[rows]
        mixed = bias
        for h in range(N_HEADS):
            mixed = mixed + _dot(ws[h], jnp.where(head == h, vc, 0.0).astype(BF16))
        out.append((u[rows] * mixed).astype(BF16))
    return jnp.concatenate(out, axis=0)


def _hg_constants():
    t = np.arange(HG_TILE)
    same = (t[:, None] // HG_GROUP) == (t[None, :] // HG_GROUP)
    m_cum = (same & (t[None, :] <= t[:, None])).astype(np.float32)
    kg = np.arange(HG_KW)
    e = np.arange(GROUP_WIDTH)
    ind = (kg[:, None] // HG_KEY == e[None, :] // HEAD_V).astype(np.float32)
    return jnp.asarray(np.tile(m_cum, (1, 3)), dtype=BF16), jnp.asarray(ind, dtype=BF16)


def _head_avg3():
    e = np.arange(GROUP_WIDTH)
    avg = (e[:, None] // HEAD_V == e[None, :] // HEAD_V).astype(np.float32) / HEAD_V
    return jnp.asarray(np.tile(avg, (3, 1)), dtype=BF16)


def _hgrn2(z, lb_ref, ng_ref, mcum3_ref, ind_ref, avg3_ref, st):
    TT, KW = HG_TILE, HG_KW
    n_grp, n_sub = TT // HG_GROUP, HG_GROUP // HG_SUB
    q = z[:, :KW]
    f = z[:, KW:2 * KW]
    vi = z[:, 2 * KW:2 * KW + GROUP_WIDTH]
    g = z[:, 2 * KW + GROUP_WIDTH:]
    lb = lb_ref[...]
    forget = lb + (1.0 - lb) * jax.nn.sigmoid(f)
    kk = 1.0 - forget
    b = _sum_rows_f32(mcum3_ref, jnp.log(forget))
    head_v = lax.broadcasted_iota(jnp.int32, (1, GROUP_WIDTH), 1) // HEAD_V
    ind = ind_ref[...]

    tiles = lambda a: a.reshape(TT // HG_SUB, HG_SUB, a.shape[-1])
    q3, f3, v3 = tiles(q), tiles(forget), tiles(vi)
    r_sub = lax.broadcasted_iota(jnp.int32, (1, HG_SUB, 1), 1)
    o = _dot((q * kk).astype(BF16), ind) * vi
    decay, f_prev = None, f3
    for d in range(1, HG_SUB):
        f_d = pltpu.roll(f3, d, 1)
        decay = f3 if d == 1 else decay * f_prev
        x = jnp.where(r_sub >= d, q3 * (1.0 - f_d) * decay, 0.0)
        w = _dot(x.reshape(TT, KW).astype(BF16), ind)
        o = o + w * pltpu.roll(v3, d, 1).reshape(TT, GROUP_WIDTH)
        f_prev = f_d

    groups = lambda a: a.reshape(n_grp, n_sub, HG_SUB, a.shape[-1])
    b4, kk4, q4 = groups(b), groups(kk), groups(q)
    ends = jnp.broadcast_to(b4[:, :, HG_SUB - 1:HG_SUB, :], b4.shape)
    bsub4 = jnp.concatenate([jnp.zeros_like(ends[:, :1]), ends[:, :n_sub - 1]], axis=1)
    blast4 = ends[:, n_sub - 1:]

    qp4 = q4 * jnp.exp(b4 - bsub4)
    zeros4 = lambda n: [jnp.zeros((n_grp, n, HG_SUB, KW), F32)] if n else []
    q_slots, k_slots = [], []
    for i in range(1, n_sub):
        k_i = kk4[:, :i] * jnp.exp(bsub4[:, i:i + 1] - b4[:, :i])
        k_slots.append(jnp.concatenate([k_i] + zeros4(n_sub - i), axis=1).reshape(TT, KW))
        q_slots.append(jnp.concatenate(zeros4(i) + [qp4[:, i:i + 1]] + zeros4(n_sub - 1 - i),
                                       axis=1).reshape(TT, KW))
    same_group = ((lax.broadcasted_iota(jnp.int32, (TT, TT), 0) // HG_GROUP)
                  == (lax.broadcasted_iota(jnp.int32, (TT, TT), 1) // HG_GROUP))
    for h in range(N_HEADS):
        lanes = slice(h * HG_KEY, (h + 1) * HG_KEY)
        lhs = jnp.concatenate([s[:, lanes] for s in q_slots], axis=1).astype(BF16)
        rhs = jnp.concatenate([s[:, lanes] for s in k_slots], axis=1).astype(BF16)
        a = lax.dot_general(lhs, rhs, NT_DIMS, preferred_element_type=F32)
        a = jnp.where(same_group, a, 0.0).astype(BF16)
        o = o + _dot(a, jnp.where(head_v == h, vi, 0.0).astype(BF16))

    qe = (q * jnp.exp(b)).astype(BF16)
    k_dec = (kk4 * jnp.exp(blast4 - b4)).reshape(TT, KW).astype(BF16)
    e_last = jnp.exp(blast4[:, 0, 0:1, :])
    vb = vi.astype(BF16)
    same_head = (lax.broadcasted_iota(jnp.int32, (GROUP_WIDTH, 1), 0) // HEAD_V
                 == lax.broadcasted_iota(jnp.int32, (1, KW), 1) // HG_KEY)
    inter = []
    for gi in range(n_grp):
        rows = slice(gi * HG_GROUP, (gi + 1) * HG_GROUP)
        inter.append(lax.dot_general(qe[rows], st.astype(BF16), NT_DIMS, preferred_element_type=F32))
        upd = lax.dot_general(vb[rows], k_dec[rows], TN_DIMS, preferred_element_type=F32)
        st = st * e_last[gi] + jnp.where(same_head, upd, 0.0)
    o = o + jnp.concatenate(inter, axis=0)

    ms = _sum_lanes_f32(o * o, avg3_ref)
    return (o * lax.rsqrt(ms + EPS) * ng_ref[...] * _silu(g)).astype(BF16), st


def _ret_constants():
    C = RET_CHUNK
    lg = np.log1p(-np.power(2.0, -(5.0 + np.arange(N_HEADS, dtype=np.float64))))
    pos = np.arange(C, dtype=np.float64)
    rel = pos[:, None] - pos[None, :]
    decay = np.where(rel >= 0, np.exp(np.maximum(rel, 0.0)[None] * lg[:, None, None]), 0.0)
    lane = np.arange(GROUP_WIDTH)
    lg_qk = lg[(lane % 128) // (RET_QK // 2)]
    lg_v = lg[lane // HEAD_V]
    gq = np.exp((pos + 1.0)[:, None] * lg_qk[None, :])
    gk = np.exp((C - 1.0 - pos)[:, None] * lg_qk[None, :])
    gc = np.exp(C * lg_v)[None, :]
    f = lambda a: jnp.asarray(a.astype(np.float32))
    return f(decay), f(gq), f(gk), f(gc)


def _retention(z, c, s, dec_ref, gq_ref, gk_ref, gc_ref, ng_ref, avg3_ref, st):
    W = GROUP_WIDTH
    q1, q2 = z[:, 0:128], z[:, 128:256]
    k1, k2 = z[:, 256:384], z[:, 384:512]
    v = z[:, 2 * W:3 * W]
    g = z[:, 3 * W:]
    qr = jnp.concatenate([q1 * c - q2 * s, q1 * s + q2 * c], axis=1)
    kr = jnp.concatenate([k1 * c - k2 * s, k1 * s + k2 * c], axis=1) * (RET_QK ** -0.5)
    lane = lax.broadcasted_iota(jnp.int32, (1, W), 1)
    head_qk = (lane & 127) // (RET_QK // 2)
    head_v = lane // HEAD_V
    krb = kr.astype(BF16)
    vb = v.astype(BF16)

    o = _dot((qr * gq_ref[...]).astype(BF16), st.astype(BF16))
    for h in range(N_HEADS):
        qm = jnp.where(head_qk == h, qr, 0.0).astype(BF16)
        sc = lax.dot_general(qm, krb, NT_DIMS, preferred_element_type=F32) * dec_ref[h]
        o = o + _dot(sc.astype(BF16), jnp.where(head_v == h, vb, jnp.zeros_like(vb)))

    upd = lax.dot_general((kr * gk_ref[...]).astype(BF16), vb, TN_DIMS, preferred_element_type=F32)
    row_head = (lax.broadcasted_iota(jnp.int32, (W, 1), 0) & 127) // (RET_QK // 2)
    st = st * gc_ref[...] + jnp.where(row_head == head_v, upd, 0.0)

    oc = o - _sum_lanes_f32(o, avg3_ref)
    var = _sum_lanes_f32(oc * oc, avg3_ref)
    return (oc * lax.rsqrt(var + EPS) * ng_ref[...] * _silu(g)).astype(BF16), st


def _mix_kernel(tiles_per_seq,
                h_ref, cb_ref, su_ref, sd_ref, cos_ref, sin_ref,
                g1_ref, wg_ref, wu_ref, wd_ref, g2_ref, win_ref,
                qn_ref, kvn_ref, wuq_ref, wukv_ref,
                sgln_ref, sgw_ref, sgb_ref,
                lb_ref, hgn_ref, mcum3_ref, ind_ref, avg3_ref,
                dec_ref, gq_ref, gk_ref, gc_ref, retn_ref,
                h_out, q_out, k_out, v_out, yb_out, yc_out, yd_out,
                hg_st, ret_st):
    @pl.when(pl.program_id(0) % tiles_per_seq == 0)
    def _():
        hg_st[...] = jnp.zeros_like(hg_st)
        ret_st[...] = jnp.zeros_like(ret_st)

    h1 = _ffn_half_step(h_ref[...], g1_ref, wg_ref, wu_ref, wd_ref)
    h_out[...] = h1
    x = _rms(h1, g2_ref[...]).astype(BF16)
    zs = []
    off = 0
    for w in (ZM_W, ZS_W, ZH_W, ZR_W):
        zs.append(_dot(x, win_ref[:, off:off + w]))
        off += w
    zm, zg, zh, zr = zs

    q, k, v = _mla_prep(zm, qn_ref, kvn_ref, wuq_ref, wukv_ref, cb_ref[...], su_ref[...], sd_ref[...])
    q_out[...] = q
    k_out[...] = k
    v_out[...] = v
    yb_out[...] = _spatial_gating(zg, sgln_ref, sgw_ref, sgb_ref)

    st = hg_st[...]
    for n in range(TOKEN_TILE // HG_TILE):
        rows = slice(n * HG_TILE, (n + 1) * HG_TILE)
        y, st = _hgrn2(zh[rows], lb_ref, hgn_ref, mcum3_ref, ind_ref, avg3_ref, st)
        yc_out[rows, :] = y
    hg_st[...] = st

    st = ret_st[...]
    cos, sin = cos_ref[...], sin_ref[...]
    for n in range(TOKEN_TILE // RET_CHUNK):
        rows = slice(n * RET_CHUNK, (n + 1) * RET_CHUNK)
        y, st = _retention(zr[rows], cos[rows], sin[rows], dec_ref, gq_ref, gk_ref, gc_ref, retn_ref,
                           avg3_ref, st)
        yd_out[rows, :] = y
    ret_st[...] = st


def _mix_call(h, tables, weights, S):
    T = h.shape[0]
    tm = TOKEN_TILE
    tok = lambda w: pl.BlockSpec((tm, w), lambda i: (i, 0))
    out_w = (D_MODEL, MLA_W, MLA_W, MLA_W, GROUP_WIDTH, GROUP_WIDTH, GROUP_WIDTH)
    out_t = (F32,) + (BF16,) * 6
    return pl.pallas_call(
        functools.partial(_mix_kernel, S // tm),
        grid=(T // tm,),
        in_specs=[tok(D_MODEL)] + [tok(128)] * 5 + [_weight_spec(w) for w in weights],
        out_specs=[tok(w) for w in out_w],
        out_shape=[jax.ShapeDtypeStruct((T, w), t) for w, t in zip(out_w, out_t)],
        scratch_shapes=[pltpu.VMEM((GROUP_WIDTH, HG_KW), F32), pltpu.VMEM((GROUP_WIDTH, GROUP_WIDTH), F32)],
        compiler_params=_params("arbitrary"),
        name="ffn_inproj_mixers",
    )(h, *tables, *[_weight_arg(w) for w in weights])


def _post_kernel(final, h_ref, ya_ref, yb_ref, yc_ref, yd_ref, p_ref, wo_ref, g_ref, wg_ref, wu_ref,
                 wd_ref, gp_ref, wpp_ref, wpg_ref, gf_ref, out_ref):
    h = h_ref[...]
    for n, y_ref in enumerate((ya_ref, yb_ref, yc_ref, yd_ref)):
        h = h + _dot(y_ref[...], wo_ref[n * GROUP_WIDTH:(n + 1) * GROUP_WIDTH, :])
    h = _ffn_half_step(h, g_ref, wg_ref, wu_ref, wd_ref)
    gate = jax.nn.sigmoid(_dot(_rms(h, gp_ref[...]).astype(BF16), wpg_ref[...]))
    h = h + _dot(p_ref[...].astype(BF16), wpp_ref[...]) * gate
    if final:
        h = _rms(h, gf_ref[...])
    out_ref[...] = h


def _post_call(h, ys, p, weights, final):
    T = h.shape[0]
    tm = TOKEN_TILE
    tok = lambda w: pl.BlockSpec((tm, w), lambda i: (i, 0))
    p_spec = pl.BlockSpec((None, tm, PLE_DIM), lambda i: (p.index, i, 0))
    return pl.pallas_call(
        functools.partial(_post_kernel, final),
        grid=(T // tm,),
        in_specs=[tok(D_MODEL)] + [tok(GROUP_WIDTH)] * 4 + [p_spec] + [_weight_spec(w) for w in weights],
        out_specs=tok(D_MODEL),
        out_shape=jax.ShapeDtypeStruct((T, D_MODEL), F32),
        compiler_params=_params("parallel"),
        name="post_outproj_ffn_ple",
    )(h, *ys, p.stacked, *[_weight_arg(w) for w in weights])


def _prep_w_in(w):
    cuts = np.cumsum([0, 256, 128, 32, 256, 256, 512, 512, 256, 256, 256, 256, 256, 256])
    c_q, c_kv, k_pe, z_u, z_v, hq, hf, hi, hg, rq, rk, rv, rg = (
        w[..., cuts[n]:cuts[n + 1]] for n in range(13))
    zeros = lambda n: jnp.zeros(w.shape[:-1] + (n,), w.dtype)
    half = MLA_ROPE // 2
    mla = [c_q, c_kv, zeros(MLA_NOPE), k_pe[..., :half], k_pe[..., half:],
           zeros(MLA_HEAD_PAD - MLA_NOPE - MLA_ROPE)]

    def split_halves(m):
        lead = m.shape[:-1]
        return m.reshape(lead + (N_HEADS, 2, RET_QK // 2)).swapaxes(-3, -2).reshape(lead + (N_HEADS * RET_QK,))

    cols = mla + [z_u, z_v, hq, hf, hi, hg, split_halves(rq), split_halves(rk), rv, rg]
    return jnp.concatenate(cols, axis=-1).astype(BF16)


def _prep_w_uq(w):
    w = w.reshape(MLA_Q_LORA, N_HEADS, MLA_NOPE + MLA_ROPE)
    w = jnp.pad(w, ((0, 0), (0, 0), (0, MLA_HEAD_PAD - MLA_NOPE - MLA_ROPE)))
    return w.reshape(MLA_Q_LORA, MLA_W).astype(BF16)


def _prep_w_ukv(w):
    w = w.reshape(MLA_KV_LORA, N_HEADS, MLA_NOPE + HEAD_V)
    pad = lambda m: jnp.pad(m, ((0, 0), (0, 0), (0, MLA_HEAD_PAD - m.shape[-1]))).reshape(MLA_KV_LORA, -1)
    return jnp.concatenate([pad(w[..., :MLA_NOPE]), pad(w[..., MLA_NOPE:])], axis=1).astype(BF16)


def _rope_tables(positions):
    pos = positions.astype(F32).reshape(-1)[:, None]

    def inv_freq(d):
        return ROPE_THETA ** (-jnp.arange(0, d, 2, dtype=F32) / d)

    half = MLA_ROPE // 2
    lane = np.arange(MLA_HEAD_PAD)
    x1 = ((lane >= MLA_NOPE) & (lane < MLA_NOPE + half)).astype(np.float32)
    x2 = ((lane >= MLA_NOPE + half) & (lane < MLA_NOPE + MLA_ROPE)).astype(np.float32)
    nope = (lane < MLA_NOPE).astype(np.float32)
    freq = jnp.zeros((MLA_HEAD_PAD,), F32)
    freq = freq.at[MLA_NOPE:MLA_NOPE + half].set(inv_freq(MLA_ROPE))
    freq = freq.at[MLA_NOPE + half:MLA_NOPE + MLA_ROPE].set(inv_freq(MLA_ROPE))
    ang = pos * freq[None, :]
    c, s = jnp.cos(ang), jnp.sin(ang)
    cb = c * (x1 + x2)[None, :] + nope[None, :]
    su = s * x2[None, :]
    sd = -s * x1[None, :]
    ang_r = pos * jnp.tile(inv_freq(RET_QK), N_HEADS)[None, :]
    return cb, su, sd, jnp.cos(ang_r), jnp.sin(ang_r)


def kernel(x, p, positions, ffn1_norm, ffn1_w_gate, ffn1_w_up, ffn1_w_down, mix_norm, w_in, mla_q_norm, mla_w_uq, mla_kv_norm, mla_w_ukv, sg_ln, sg_w_s, sg_b_s, hg_lb_logits, hg_norm, ret_norm, w_out, ffn2_norm, ffn2_w_gate, ffn2_w_up, ffn2_w_down, ple_norm, ple_w_proj, ple_w_gate, final_norm):
    B, S, D = x.shape
    L = w_in.shape[0]
    T = B * S
    row = lambda a: a.reshape(1, -1).astype(F32)

    tables = _rope_tables(positions)
    mcum3, ind = _hg_constants()
    dec, gq, gk, gc = _ret_constants()
    avg3 = _head_avg3()
    lb_all = jnp.cumsum(jax.nn.softmax(hg_lb_logits.astype(F32), axis=0), axis=0)
    lb_all = lb_all - lb_all[0:1]
    gf = row(final_norm)
    bf16 = lambda a: a.astype(BF16)
    ffn1, ffn2 = [tuple(bf16(w) for w in ws) for ws in ((ffn1_w_gate, ffn1_w_up, ffn1_w_down),
                                                       (ffn2_w_gate, ffn2_w_up, ffn2_w_down))]
    win_all, wo_all, wpp_all, wpg_all = _prep_w_in(w_in), bf16(w_out), bf16(ple_w_proj), bf16(ple_w_gate)
    p_all = p.reshape(L, T, PLE_DIM)

    h = x.reshape(T, D)
    for l in range(L):
        sg_bias = jnp.repeat(sg_b_s[l].T, HEAD_V, axis=1)
        weights = (row(ffn1_norm[l]), *[_Layer(w, l) for w in ffn1],
                   row(mix_norm[l]), _Layer(win_all, l),
                   row(mla_q_norm[l]), row(mla_kv_norm[l]), _prep_w_uq(mla_w_uq[l]), _prep_w_ukv(mla_w_ukv[l]),
                   row(sg_ln[l]), sg_w_s[l], sg_bias,
                   row(lb_all[l]), row(hg_norm[l]), mcum3, ind, avg3,
                   dec, gq, gk, gc, row(ret_norm[l]))
        h, q, k, v, y_b, y_c, y_d = _mix_call(h, tables, weights, S)
        y_a = _attn_call(q, k, v, B, S)
        weights = (_Layer(wo_all, l), row(ffn2_norm[l]), *[_Layer(w, l) for w in ffn2],
                   row(ple_norm[l]), _Layer(wpp_all, l), _Layer(wpg_all, l), gf)
        h = _post_call(h, (y_a, y_b, y_c, y_d), _Layer(p_all, l), weights, final=(l == L - 1))
    return h.reshape(B, S, D)
```

```python
import functools
from typing import NamedTuple

import numpy as np
import jax
import jax.numpy as jnp
from jax import lax
from jax.experimental import pallas as pl
from jax.experimental.pallas import tpu as pltpu

F32 = jnp.float32
BF16 = jnp.bfloat16
EPS = 1e-6
ROPE_THETA = 10000.0

LANE = 128
D_MODEL = 1024
D_FF = 2816
FF_CHUNK = 256
N_FF_CHUNKS = D_FF // FF_CHUNK
PLE_DIM = 256
GROUP_WIDTH = 256
N_HEADS = 4
HEAD_V = GROUP_WIDTH // N_HEADS

MLA_NOPE, MLA_ROPE, MLA_Q_LORA, MLA_KV_LORA = 64, 32, 256, 128
MLA_HEAD_PAD = 128
MLA_W = N_HEADS * MLA_HEAD_PAD
SG_CHUNK = 128
HG_KEY = 128
HG_KW = N_HEADS * HG_KEY
HG_SUB = 8
HG_GROUP = 64
HG_TILE = 256
RET_QK = 64
RET_CHUNK = 256

ZM_W, ZS_W, ZH_W, ZR_W = 512, 512, 1536, 1024

TOKEN_TILE = 512
ATTN_Q_TILE = 1024
ATTN_KV_TILE = 512
ATTN_ROW_BLOCK = 32
VMEM_LIMIT = 60 * 1024 * 1024

NT_DIMS = (((1,), (1,)), ((), ()))
TN_DIMS = (((0,), (0,)), ((), ()))


def _rms(x, g):
    return x * lax.rsqrt(jnp.mean(x * x, axis=-1, keepdims=True) + EPS) * g


def _silu(x):
    return x * jax.nn.sigmoid(x)


def _dot(a, b):
    return jnp.dot(a, b, preferred_element_type=F32)


def _params(*sem):
    return pltpu.CompilerParams(dimension_semantics=sem, vmem_limit_bytes=VMEM_LIMIT)


def _full(shape):
    return pl.BlockSpec(shape, lambda *_: (0,) * len(shape))


class _Layer(NamedTuple):
    stacked: jax.Array
    index: int


def _weight_spec(w):
    if isinstance(w, _Layer):
        shape = w.stacked.shape[1:]
        return pl.BlockSpec((None,) + shape, lambda *_: (w.index,) + (0,) * len(shape),
                            pipeline_mode=pl.Buffered(1))
    return _full(w.shape)


def _weight_arg(w):
    return w.stacked if isinstance(w, _Layer) else w


def _split3(x):
    p1 = x.astype(BF16)
    r = x - p1.astype(F32)
    p2 = r.astype(BF16)
    p3 = (r - p2.astype(F32)).astype(BF16)
    return p1, p2, p3


def _sum_rows_f32(m3_ref, x):
    return _dot(m3_ref[...], jnp.concatenate(_split3(x), axis=0))


def _sum_lanes_f32(x, m3_ref):
    return _dot(jnp.concatenate(_split3(x), axis=1), m3_ref[...])


def _ffn_half_step(h, g_ref, wg_ref, wu_ref, wd_ref):
    x = _rms(h, g_ref[...]).astype(BF16)
    acc = jnp.zeros_like(h)
    for c in range(N_FF_CHUNKS):
        cols = slice(c * FF_CHUNK, (c + 1) * FF_CHUNK)
        a = (_silu(_dot(x, wg_ref[:, cols])) * _dot(x, wu_ref[:, cols])).astype(BF16)
        acc = acc + _dot(a, wd_ref[cols, :])
    return h + 0.5 * acc


def _rope_lanes(x, cb, su, sd):
    w = x.shape[1]
    half = MLA_ROPE // 2
    return x * cb + pltpu.roll(x, half, 1) * su + pltpu.roll(x, w - half, 1) * sd


def _mla_prep(z, qn_ref, kvn_ref, wuq_ref, wukv_ref, cb, su, sd):
    c_q = z[:, :MLA_Q_LORA]
    c_kv = z[:, MLA_Q_LORA:MLA_Q_LORA + MLA_KV_LORA]
    k_pe = z[:, MLA_Q_LORA + MLA_KV_LORA:]
    q = _dot(_rms(c_q, qn_ref[...]).astype(BF16), wuq_ref[...])
    kv = _dot(_rms(c_kv, kvn_ref[...]).astype(BF16), wukv_ref[...])
    tile4 = lambda t: jnp.concatenate([t] * N_HEADS, axis=1)
    scale = (MLA_NOPE + MLA_ROPE) ** -0.5
    q = (_rope_lanes(q, tile4(cb), tile4(su), tile4(sd)) * scale).astype(BF16)
    k = (kv[:, :MLA_W] + tile4(_rope_lanes(k_pe, cb, su, sd))).astype(BF16)
    return q, k, kv[:, MLA_W:].astype(BF16)


def _attn_kernel(q_ref, k_ref, v_ref, o_ref, s_ref, p_ref, m_ref, l_ref, a_ref, acc_ref):
    i = pl.program_id(1)
    tq, tk, rb = ATTN_Q_TILE, ATTN_KV_TILE, ATTN_ROW_BLOCK
    head_lanes = [slice(h * MLA_HEAD_PAD, (h + 1) * MLA_HEAD_PAD) for h in range(N_HEADS)]
    m_ref[...] = jnp.full(m_ref.shape, -jnp.inf, F32)
    l_ref[...] = jnp.zeros(l_ref.shape, F32)
    acc_ref[...] = jnp.zeros(acc_ref.shape, F32)
    row = lax.broadcasted_iota(jnp.int32, (rb, tk), 0)
    col = lax.broadcasted_iota(jnp.int32, (rb, tk), 1)

    def step(j, shift):
        start = pl.multiple_of(j * tk, tk)
        first = 0 if shift is None else shift
        live = slice(first, tq)
        for h, lanes in enumerate(head_lanes):
            s_ref[h, live, :] = lax.dot_general(q_ref[live, lanes], k_ref[pl.ds(start, tk), lanes], NT_DIMS,
                                                preferred_element_type=F32)
            for r in range(first // rb, tq // rb):
                rows = slice(r * rb, (r + 1) * rb)
                s = s_ref[h, rows, :]
                if shift is not None and r * rb < shift + tk - 1:
                    s = jnp.where(row + r * rb >= col + shift, s, -jnp.inf)
                tiles = [s[:, c * LANE:(c + 1) * LANE] for c in range(tk // LANE)]
                m_old = m_ref[h, rows, :]
                m_new = jnp.maximum(m_old, jnp.max(functools.reduce(jnp.maximum, tiles), axis=1, keepdims=True))
                p_tiles = [jnp.exp(t - m_new) for t in tiles]
                alpha = jnp.exp(m_old - m_new)
                l_ref[h, rows, :] = alpha * l_ref[h, rows, :] + jnp.sum(functools.reduce(jnp.add, p_tiles),
                                                                      axis=1, keepdims=True)
                m_ref[h, rows, :] = m_new
                a_ref[h, rows, :] = alpha
                p_ref[h, rows, :] = jnp.concatenate([t.astype(BF16) for t in p_tiles], axis=1)
            acc_ref[h, live, :] = (a_ref[h, live, :] * acc_ref[h, live, :]
                                   + _dot(p_ref[h, live, :], v_ref[pl.ds(start, tk), lanes]))

    per_q = tq // tk

    def body(j, carry):
        step(j, None)
        return carry

    lax.fori_loop(0, i * per_q, body, 0)
    for d in range(per_q):
        step(i * per_q + d, d * tk)
    heads = [acc_ref[h] / l_ref[h] for h in range(N_HEADS)]
    lane = lax.broadcasted_iota(jnp.int32, (tq, MLA_HEAD_PAD), 1)
    lo = lane < HEAD_V
    pair = lambda a, b: jnp.where(lo, a, pltpu.roll(b, HEAD_V, 1))
    o_ref[...] = jnp.concatenate([pair(heads[0], heads[1]), pair(heads[2], heads[3])],
                                 axis=1).astype(o_ref.dtype)


def _attn_call(q, k, v, B, S):
    nq = S // ATTN_Q_TILE
    tq, tk = ATTN_Q_TILE, ATTN_KV_TILE
    stat = pltpu.VMEM((N_HEADS, tq, LANE), F32)
    return pl.pallas_call(
        _attn_kernel,
        grid=(B, nq),
        in_specs=[pl.BlockSpec((tq, MLA_W), lambda b, i: (b * nq + i, 0)),
                  pl.BlockSpec((S, MLA_W), lambda b, i: (b, 0)),
                  pl.BlockSpec((S, MLA_W), lambda b, i: (b, 0))],
        out_specs=pl.BlockSpec((tq, GROUP_WIDTH), lambda b, i: (b * nq + i, 0)),
        out_shape=jax.ShapeDtypeStruct((B * S, GROUP_WIDTH), BF16),
        scratch_shapes=[pltpu.VMEM((N_HEADS, tq, tk), F32), pltpu.VMEM((N_HEADS, tq, tk), BF16),
                        stat, stat, stat, stat],
        compiler_params=_params("parallel", "arbitrary"),
        name="mla_attention",
    )(q, k, v)


def _gelu(x):
    return 0.5 * x * (1.0 + lax.erf(x * np.float32(np.sqrt(0.5))))


def _spatial_gating(z, ln_ref, w_ref, bias_ref):
    u = _gelu(z[:, :GROUP_WIDTH])
    gv = _gelu(z[:, GROUP_WIDTH:])
    xc = gv - jnp.mean(gv, axis=-1, keepdims=True)
    v = xc * lax.rsqrt(jnp.mean(xc * xc, axis=-1, keepdims=True) + EPS) * ln_ref[...]
    C = SG_CHUNK
    causal = (lax.broadcasted_iota(jnp.int32, (C, C), 0) >= lax.broadcasted_iota(jnp.int32, (C, C), 1))
    head = lax.broadcasted_iota(jnp.int32, (1, GROUP_WIDTH), 1) // HEAD_V
    ws = [jnp.where(causal, w_ref[h], 0.0).astype(BF16) for h in range(N_HEADS)]
    bias = bias_ref[...]
    out = []
    for c in range(z.shape[0] // C):
        rows = slice(c * C, (c + 1) * C)
        vc = v[rows]
        mixed = bias
        for h in range(N_HEADS):
            mixed = mixed + _dot(ws[h], jnp.where(head == h, vc, 0.0).astype(BF16))
        out.append((u[rows] * mixed).astype(BF16))
    return jnp.concatenate(out, axis=0)


def _hg_constants():
    t = np.arange(HG_TILE)
    same = (t[:, None] // HG_GROUP) == (t[None, :] // HG_GROUP)
    m_cum = (same & (t[None, :] <= t[:, None])).astype(np.float32)
    kg = np.arange(HG_KW)
    e = np.arange(GROUP_WIDTH)
    ind = (kg[:, None] // HG_KEY == e[None, :] // HEAD_V).astype(np.float32)
    return jnp.asarray(np.tile(m_cum, (1, 3)), dtype=BF16), jnp.asarray(ind, dtype=BF16)


def _head_avg3():
    e = np.arange(GROUP_WIDTH)
    avg = (e[:, None] // HEAD_V == e[None, :] // HEAD_V).astype(np.float32) / HEAD_V
    return jnp.asarray(np.tile(avg, (3, 1)), dtype=BF16)


def _hgrn2(z, lb_ref, ng_ref, mcum3_ref, ind_ref, avg3_ref, st):
    TT, KW = HG_TILE, HG_KW
    n_grp, n_sub = TT // HG_GROUP, HG_GROUP // HG_SUB
    q = z[:, :KW]
    f = z[:, KW:2 * KW]
    vi = z[:, 2 * KW:2 * KW + GROUP_WIDTH]
    g = z[:, 2 * KW + GROUP_WIDTH:]
    lb = lb_ref[...]
    forget = lb + (1.0 - lb) * jax.nn.sigmoid(f)
    kk = 1.0 - forget
    b = _sum_rows_f32(mcum3_ref, jnp.log(forget))
    head_v = lax.broadcasted_iota(jnp.int32, (1, GROUP_WIDTH), 1) // HEAD_V
    ind = ind_ref[...]

    tiles = lambda a: a.reshape(TT // HG_SUB, HG_SUB, a.shape[-1])
    q3, f3, v3 = tiles(q), tiles(forget), tiles(vi)
    r_sub = lax.broadcasted_iota(jnp.int32, (1, HG_SUB, 1), 1)
    o = _dot((q * kk).astype(BF16), ind) * vi
    decay, f_prev = None, f3
    for d in range(1, HG_SUB):
        f_d = pltpu.roll(f3, d, 1)
        decay = f3 if d == 1 else decay * f_prev
        x = jnp.where(r_sub >= d, q3 * (1.0 - f_d) * decay, 0.0)
        w = _dot(x.reshape(TT, KW).astype(BF16), ind)
        o = o + w * pltpu.roll(v3, d, 1).reshape(TT, GROUP_WIDTH)
        f_prev = f_d

    groups = lambda a: a.reshape(n_grp, n_sub, HG_SUB, a.shape[-1])
    b4, kk4, q4 = groups(b), groups(kk), groups(q)
    ends = jnp.broadcast_to(b4[:, :, HG_SUB - 1:HG_SUB, :], b4.shape)
    bsub4 = jnp.concatenate([jnp.zeros_like(ends[:, :1]), ends[:, :n_sub - 1]], axis=1)
    blast4 = ends[:, n_sub - 1:]

    qp4 = q4 * jnp.exp(b4 - bsub4)
    zeros4 = lambda n: [jnp.zeros((n_grp, n, HG_SUB, KW), F32)] if n else []
    q_slots, k_slots = [], []
    for i in range(1, n_sub):
        k_i = kk4[:, :i] * jnp.exp(bsub4[:, i:i + 1] - b4[:, :i])
        k_slots.append(jnp.concatenate([k_i] + zeros4(n_sub - i), axis=1).reshape(TT, KW))
        q_slots.append(jnp.concatenate(zeros4(i) + [qp4[:, i:i + 1]] + zeros4(n_sub - 1 - i),
                                       axis=1).reshape(TT, KW))
    same_group = ((lax.broadcasted_iota(jnp.int32, (TT, TT), 0) // HG_GROUP)
                  == (lax.broadcasted_iota(jnp.int32, (TT, TT), 1) // HG_GROUP))
    for h in range(N_HEADS):
        lanes = slice(h * HG_KEY, (h + 1) * HG_KEY)
        lhs = jnp.concatenate([s[:, lanes] for s in q_slots], axis=1).astype(BF16)
        rhs = jnp.concatenate([s[:, lanes] for s in k_slots], axis=1).astype(BF16)
        a = lax.dot_general(lhs, rhs, NT_DIMS, preferred_element_type=F32)
        a = jnp.where(same_group, a, 0.0).astype(BF16)
        o = o + _dot(a, jnp.where(head_v == h, vi, 0.0).astype(BF16))

    qe = (q * jnp.exp(b)).astype(BF16)
    k_dec = (kk4 * jnp.exp(blast4 - b4)).reshape(TT, KW).astype(BF16)
    e_last = jnp.exp(blast4[:, 0, 0:1, :])
    vb = vi.astype(BF16)
    same_head = (lax.broadcasted_iota(jnp.int32, (GROUP_WIDTH, 1), 0) // HEAD_V
                 == lax.broadcasted_iota(jnp.int32, (1, KW), 1) // HG_KEY)
    inter = []
    for gi in range(n_grp):
        rows = slice(gi * HG_GROUP, (gi + 1) * HG_GROUP)
        inter.append(lax.dot_general(qe[rows], st.astype(BF16), NT_DIMS, preferred_element_type=F32))
        upd = lax.dot_general(vb[rows], k_dec[rows], TN_DIMS, preferred_element_type=F32)
        st = st * e_last[gi] + jnp.where(same_head, upd, 0.0)
    o = o + jnp.concatenate(inter, axis=0)

    ms = _sum_lanes_f32(o * o, avg3_ref)
    return (o * lax.rsqrt(ms + EPS) * ng_ref[...] * _silu(g)).astype(BF16), st


def _ret_constants():
    C = RET_CHUNK
    lg = np.log1p(-np.power(2.0, -(5.0 + np.arange(N_HEADS, dtype=np.float64))))
    pos = np.arange(C, dtype=np.float64)
    rel = pos[:, None] - pos[None, :]
    decay = np.where(rel >= 0, np.exp(np.maximum(rel, 0.0)[None] * lg[:, None, None]), 0.0)
    lane = np.arange(GROUP_WIDTH)
    lg_qk = lg[(lane % 128) // (RET_QK // 2)]
    lg_v = lg[lane // HEAD_V]
    gq = np.exp((pos + 1.0)[:, None] * lg_qk[None, :])
    gk = np.exp((C - 1.0 - pos)[:, None] * lg_qk[None, :])
    gc = np.exp(C * lg_v)[None, :]
    f = lambda a: jnp.asarray(a.astype(np.float32))
    return f(decay), f(gq), f(gk), f(gc)


def _retention(z, c, s, dec_ref, gq_ref, gk_ref, gc_ref, ng_ref, avg3_ref, st):
    W = GROUP_WIDTH
    q1, q2 = z[:, 0:128], z[:, 128:256]
    k1, k2 = z[:, 256:384], z[:, 384:512]
    v = z[:, 2 * W:3 * W]
    g = z[:, 3 * W:]
    qr = jnp.concatenate([q1 * c - q2 * s, q1 * s + q2 * c], axis=1)
    kr = jnp.concatenate([k1 * c - k2 * s, k1 * s + k2 * c], axis=1) * (RET_QK ** -0.5)
    lane = lax.broadcasted_iota(jnp.int32, (1, W), 1)
    head_qk = (lane & 127) // (RET_QK // 2)
    head_v = lane // HEAD_V
    krb = kr.astype(BF16)
    vb = v.astype(BF16)

    o = _dot((qr * gq_ref[...]).astype(BF16), st.astype(BF16))
    for h in range(N_HEADS):
        qm = jnp.where(head_qk == h, qr, 0.0).astype(BF16)
        sc = lax.dot_general(qm, krb, NT_DIMS, preferred_element_type=F32) * dec_ref[h]
        o = o + _dot(sc.astype(BF16), jnp.where(head_v == h, vb, jnp.zeros_like(vb)))

    upd = lax.dot_general((kr * gk_ref[...]).astype(BF16), vb, TN_DIMS, preferred_element_type=F32)
    row_head = (lax.broadcasted_iota(jnp.int32, (W, 1), 0) & 127) // (RET_QK // 2)
    st = st * gc_ref[...] + jnp.where(row_head == head_v, upd, 0.0)

    oc = o - _sum_lanes_f32(o, avg3_ref)
    var = _sum_lanes_f32(oc * oc, avg3_ref)
    return (oc * lax.rsqrt(var + EPS) * ng_ref[...] * _silu(g)).astype(BF16), st


def _mix_kernel(tiles_per_seq,
                h_ref, cb_ref, su_ref, sd_ref, cos_ref, sin_ref,
                g1_ref, wg_ref, wu_ref, wd_ref, g2_ref, win_ref,
                qn_ref, kvn_ref, wuq_ref, wukv_ref,
                sgln_ref, sgw_ref, sgb_ref,
                lb_ref, hgn_ref, mcum3_ref, ind_ref, avg3_ref,
                dec_ref, gq_ref, gk_ref, gc_ref, retn_ref,
                h_out, q_out, k_out, v_out, yb_out, yc_out, yd_out,
                hg_st, ret_st):
    @pl.when(pl.program_id(0) % tiles_per_seq == 0)
    def _():
        hg_st[...] = jnp.zeros_like(hg_st)
        ret_st[...] = jnp.zeros_like(ret_st)

    h1 = _ffn_half_step(h_ref[...], g1_ref, wg_ref, wu_ref, wd_ref)
    h_out[...] = h1
    x = _rms(h1, g2_ref[...]).astype(BF16)
    zs = []
    off = 0
    for w in (ZM_W, ZS_W, ZH_W, ZR_W):
        zs.append(_dot(x, win_ref[:, off:off + w]))
        off += w
    zm, zg, zh, zr = zs

    q, k, v = _mla_prep(zm, qn_ref, kvn_ref, wuq_ref, wukv_ref, cb_ref[...], su_ref[...], sd_ref[...])
    q_out[...] = q
    k_out[...] = k
    v_out[...] = v
    yb_out[...] = _spatial_gating(zg, sgln_ref, sgw_ref, sgb_ref)

    st = hg_st[...]
    for n in range(TOKEN_TILE // HG_TILE):
        rows = slice(n * HG_TILE, (n + 1) * HG_TILE)
        y, st = _hgrn2(zh[rows], lb_ref, hgn_ref, mcum3_ref, ind_ref, avg3_ref, st)
        yc_out[rows, :] = y
    hg_st[...] = st

    st = ret_st[...]
    cos, sin = cos_ref[...], sin_ref[...]
    for n in range(TOKEN_TILE // RET_CHUNK):
        rows = slice(n * RET_CHUNK, (n + 1) * RET_CHUNK)
        y, st = _retention(zr[rows], cos[rows], sin[rows], dec_ref, gq_ref, gk_ref, gc_ref, retn_ref,
                           avg3_ref, st)
        yd_out[rows, :] = y
    ret_st[...] = st


def _mix_call(h, tables, weights, S):
    T = h.shape[0]
    tm = TOKEN_TILE
    tok = lambda w: pl.BlockSpec((tm, w), lambda i: (i, 0))
    out_w = (D_MODEL, MLA_W, MLA_W, MLA_W, GROUP_WIDTH, GROUP_WIDTH, GROUP_WIDTH)
    out_t = (F32,) + (BF16,) * 6
    return pl.pallas_call(
        functools.partial(_mix_kernel, S // tm),
        grid=(T // tm,),
        in_specs=[tok(D_MODEL)] + [tok(128)] * 5 + [_weight_spec(w) for w in weights],
        out_specs=[tok(w) for w in out_w],
        out_shape=[jax.ShapeDtypeStruct((T, w), t) for w, t in zip(out_w, out_t)],
        scratch_shapes=[pltpu.VMEM((GROUP_WIDTH, HG_KW), F32), pltpu.VMEM((GROUP_WIDTH, GROUP_WIDTH), F32)],
        compiler_params=_params("arbitrary"),
        name="ffn_inproj_mixers",
    )(h, *tables, *[_weight_arg(w) for w in weights])


def _post_kernel(final, h_ref, ya_ref, yb_ref, yc_ref, yd_ref, p_ref, wo_ref, g_ref, wg_ref, wu_ref,
                 wd_ref, gp_ref, wpp_ref, wpg_ref, gf_ref, out_ref):
    h = h_ref[...]
    for n, y_ref in enumerate((ya_ref, yb_ref, yc_ref, yd_ref)):
        h = h + _dot(y_ref[...], wo_ref[n * GROUP_WIDTH:(n + 1) * GROUP_WIDTH, :])
    h = _ffn_half_step(h, g_ref, wg_ref, wu_ref, wd_ref)
    gate = jax.nn.sigmoid(_dot(_rms(h, gp_ref[...]).astype(BF16), wpg_ref[...]))
    h = h + _dot(p_ref[...].astype(BF16), wpp_ref[...]) * gate
    if final:
        h = _rms(h, gf_ref[...])
    out_ref[...] = h


def _post_call(h, ys, p, weights, final):
    T = h.shape[0]
    tm = TOKEN_TILE
    tok = lambda w: pl.BlockSpec((tm, w), lambda i: (i, 0))
    p_spec = pl.BlockSpec((None, tm, PLE_DIM), lambda i: (p.index, i, 0))
    return pl.pallas_call(
        functools.partial(_post_kernel, final),
        grid=(T // tm,),
        in_specs=[tok(D_MODEL)] + [tok(GROUP_WIDTH)] * 4 + [p_spec] + [_weight_spec(w) for w in weights],
        out_specs=tok(D_MODEL),
        out_shape=jax.ShapeDtypeStruct((T, D_MODEL), F32),
        compiler_params=_params("parallel"),
        name="post_outproj_ffn_ple",
    )(h, *ys, p.stacked, *[_weight_arg(w) for w in weights])


def _prep_w_in(w):
    cuts = np.cumsum([0, 256, 128, 32, 256, 256, 512, 512, 256, 256, 256, 256, 256, 256])
    c_q, c_kv, k_pe, z_u, z_v, hq, hf, hi, hg, rq, rk, rv, rg = (
        w[..., cuts[n]:cuts[n + 1]] for n in range(13))
    zeros = lambda n: jnp.zeros(w.shape[:-1] + (n,), w.dtype)
    half = MLA_ROPE // 2
    mla = [c_q, c_kv, zeros(MLA_NOPE), k_pe[..., :half], k_pe[..., half:],
           zeros(MLA_HEAD_PAD - MLA_NOPE - MLA_ROPE)]

    def split_halves(m):
        lead = m.shape[:-1]
        return m.reshape(lead + (N_HEADS, 2, RET_QK // 2)).swapaxes(-3, -2).reshape(lead + (N_HEADS * RET_QK,))

    cols = mla + [z_u, z_v, hq, hf, hi, hg, split_halves(rq), split_halves(rk), rv, rg]
    return jnp.concatenate(cols, axis=-1).astype(BF16)


def _prep_w_uq(w):
    w = w.reshape(MLA_Q_LORA, N_HEADS, MLA_NOPE + MLA_ROPE)
    w = jnp.pad(w, ((0, 0), (0, 0), (0, MLA_HEAD_PAD - MLA_NOPE - MLA_ROPE)))
    return w.reshape(MLA_Q_LORA, MLA_W).astype(BF16)


def _prep_w_ukv(w):
    w = w.reshape(MLA_KV_LORA, N_HEADS, MLA_NOPE + HEAD_V)
    pad = lambda m: jnp.pad(m, ((0, 0), (0, 0), (0, MLA_HEAD_PAD - m.shape[-1]))).reshape(MLA_KV_LORA, -1)
    return jnp.concatenate([pad(w[..., :MLA_NOPE]), pad(w[..., MLA_NOPE:])], axis=1).astype(BF16)


def _rope_tables(positions):
    pos = positions.astype(F32).reshape(-1)[:, None]

    def inv_freq(d):
        return ROPE_THETA ** (-jnp.arange(0, d, 2, dtype=F32) / d)

    half = MLA_ROPE // 2
    lane = np.arange(MLA_HEAD_PAD)
    x1 = ((lane >= MLA_NOPE) & (lane < MLA_NOPE + half)).astype(np.float32)
    x2 = ((lane >= MLA_NOPE + half) & (lane < MLA_NOPE + MLA_ROPE)).astype(np.float32)
    nope = (lane < MLA_NOPE).astype(np.float32)
    freq = jnp.zeros((MLA_HEAD_PAD,), F32)
    freq = freq.at[MLA_NOPE:MLA_NOPE + half].set(inv_freq(MLA_ROPE))
    freq = freq.at[MLA_NOPE + half:MLA_NOPE + MLA_ROPE].set(inv_freq(MLA_ROPE))
    ang = pos * freq[None, :]
    c, s = jnp.cos(ang), jnp.sin(ang)
    cb = c * (x1 + x2)[None, :] + nope[None, :]
    su = s * x2[None, :]
    sd = -s * x1[None, :]
    ang_r = pos * jnp.tile(inv_freq(RET_QK), N_HEADS)[None, :]
    return cb, su, sd, jnp.cos(ang_r), jnp.sin(ang_r)


def kernel(x, p, positions, ffn1_norm, ffn1_w_gate, ffn1_w_up, ffn1_w_down, mix_norm, w_in, mla_q_norm, mla_w_uq, mla_kv_norm, mla_w_ukv, sg_ln, sg_w_s, sg_b_s, hg_lb_logits, hg_norm, ret_norm, w_out, ffn2_norm, ffn2_w_gate, ffn2_w_up, ffn2_w_down, ple_norm, ple_w_proj, ple_w_gate, final_norm):
    B, S, D = x.shape
    L = w_in.shape[0]
    T = B * S
    row = lambda a: a.reshape(1, -1).astype(F32)

    tables = _rope_tables(positions)
    mcum3, ind = _hg_constants()
    dec, gq, gk, gc = _ret_constants()
    avg3 = _head_avg3()
    lb_all = jnp.cumsum(jax.nn.softmax(hg_lb_logits.astype(F32), axis=0), axis=0)
    lb_all = lb_all - lb_all[0:1]
    gf = row(final_norm)
    bf16 = lambda a: a.astype(BF16)
    ffn1, ffn2 = [tuple(bf16(w) for w in ws) for ws in ((ffn1_w_gate, ffn1_w_up, ffn1_w_down),
                                                       (ffn2_w_gate, ffn2_w_up, ffn2_w_down))]
    win_all, wo_all, wpp_all, wpg_all = _prep_w_in(w_in), bf16(w_out), bf16(ple_w_proj), bf16(ple_w_gate)
    p_all = p.reshape(L, T, PLE_DIM)

    h = x.reshape(T, D)
    for l in range(L):
        sg_bias = jnp.repeat(sg_b_s[l].T, HEAD_V, axis=1)
        weights = (row(ffn1_norm[l]), *[_Layer(w, l) for w in ffn1],
                   row(mix_norm[l]), _Layer(win_all, l),
                   row(mla_q_norm[l]), row(mla_kv_norm[l]), _prep_w_uq(mla_w_uq[l]), _prep_w_ukv(mla_w_ukv[l]),
                   row(sg_ln[l]), sg_w_s[l], sg_bias,
                   row(lb_all[l]), row(hg_norm[l]), mcum3, ind, avg3,
                   dec, gq, gk, gc, row(ret_norm[l]))
        h, q, k, v, y_b, y_c, y_d = _mix_call(h, tables, weights, S)
        y_a = _attn_call(q, k, v, B, S)
        weights = (_Layer(wo_all, l), row(ffn2_norm[l]), *[_Layer(w, l) for w in ffn2],
                   row(ple_norm[l]), _Layer(wpp_all, l), _Layer(wpg_all, l), gf)
        h = _post_call(h, (y_a, y_b, y_c, y_d), _Layer(p_all, l), weights, final=(l == L - 1))
    return h.reshape(B, S, D)
```

```python
import functools
from typing import NamedTuple

import numpy as np
import jax
import jax.numpy as jnp
from jax import lax
from jax.experimental import pallas as pl
from jax.experimental.pallas import tpu as pltpu

F32 = jnp.float32
BF16 = jnp.bfloat16
EPS = 1e-6
ROPE_THETA = 10000.0

LANE = 128
D_MODEL = 1024
D_FF = 2816
FF_CHUNK = 256
N_FF_CHUNKS = D_FF // FF_CHUNK
PLE_DIM = 256
GROUP_WIDTH = 256
N_HEADS = 4
HEAD_V = GROUP_WIDTH // N_HEADS

MLA_NOPE, MLA_ROPE, MLA_Q_LORA, MLA_KV_LORA = 64, 32, 256, 128
MLA_HEAD_PAD = 128
MLA_W = N_HEADS * MLA_HEAD_PAD
SG_CHUNK = 128
HG_KEY = 128
HG_KW = N_HEADS * HG_KEY
HG_SUB = 8
HG_GROUP = 64
HG_TILE = 256
RET_QK = 64
RET_CHUNK = 256

ZM_W, ZS_W, ZH_W, ZR_W = 512, 512, 1536, 1024

TOKEN_TILE = 512
ATTN_Q_TILE = 1024
ATTN_KV_TILE = 512
ATTN_ROW_BLOCK = 32
VMEM_LIMIT = 60 * 1024 * 1024

NT_DIMS = (((1,), (1,)), ((), ()))
TN_DIMS = (((0,), (0,)), ((), ()))


def _rms(x, g):
    return x * lax.rsqrt(jnp.mean(x * x, axis=-1, keepdims=True) + EPS) * g


def _silu(x):
    return x * jax.nn.sigmoid(x)


def _dot(a, b):
    return jnp.dot(a, b, preferred_element_type=F32)


def _params(*sem):
    return pltpu.CompilerParams(dimension_semantics=sem, vmem_limit_bytes=VMEM_LIMIT)


def _full(shape):
    return pl.BlockSpec(shape, lambda *_: (0,) * len(shape))


class _Layer(NamedTuple):
    stacked: jax.Array
    index: int


def _weight_spec(w):
    if isinstance(w, _Layer):
        shape = w.stacked.shape[1:]
        return pl.BlockSpec((None,) + shape, lambda *_: (w.index,) + (0,) * len(shape),
                            pipeline_mode=pl.Buffered(1))
    return _full(w.shape)


def _weight_arg(w):
    return w.stacked if isinstance(w, _Layer) else w


def _split3(x):
    p1 = x.astype(BF16)
    r = x - p1.astype(F32)
    p2 = r.astype(BF16)
    p3 = (r - p2.astype(F32)).astype(BF16)
    return p1, p2, p3


def _sum_rows_f32(m3_ref, x):
    return _dot(m3_ref[...], jnp.concatenate(_split3(x), axis=0))


def _sum_lanes_f32(x, m3_ref):
    return _dot(jnp.concatenate(_split3(x), axis=1), m3_ref[...])


def _ffn_half_step(h, g_ref, wg_ref, wu_ref, wd_ref):
    x = _rms(h, g_ref[...]).astype(BF16)
    acc = jnp.zeros_like(h)
    for c in range(N_FF_CHUNKS):
        cols = slice(c * FF_CHUNK, (c + 1) * FF_CHUNK)
        a = (_silu(_dot(x, wg_ref[:, cols])) * _dot(x, wu_ref[:, cols])).astype(BF16)
        acc = acc + _dot(a, wd_ref[cols, :])
    return h + 0.5 * acc


def _rope_lanes(x, cb, su, sd):
    w = x.shape[1]
    half = MLA_ROPE // 2
    return x * cb + pltpu.roll(x, half, 1) * su + pltpu.roll(x, w - half, 1) * sd


def _mla_prep(z, qn_ref, kvn_ref, wuq_ref, wukv_ref, cb, su, sd):
    c_q = z[:, :MLA_Q_LORA]
    c_kv = z[:, MLA_Q_LORA:MLA_Q_LORA + MLA_KV_LORA]
    k_pe = z[:, MLA_Q_LORA + MLA_KV_LORA:]
    q = _dot(_rms(c_q, qn_ref[...]).astype(BF16), wuq_ref[...])
    kv = _dot(_rms(c_kv, kvn_ref[...]).astype(BF16), wukv_ref[...])
    tile4 = lambda t: jnp.concatenate([t] * N_HEADS, axis=1)
    scale = (MLA_NOPE + MLA_ROPE) ** -0.5
    q = (_rope_lanes(q, tile4(cb), tile4(su), tile4(sd)) * scale).astype(BF16)
    k = (kv[:, :MLA_W] + tile4(_rope_lanes(k_pe, cb, su, sd))).astype(BF16)
    return q, k, kv[:, MLA_W:].astype(BF16)


def _attn_kernel(q_ref, k_ref, v_ref, o_ref, s_ref, p_ref, m_ref, l_ref, a_ref, acc_ref):
    i = pl.program_id(1)
    tq, tk, rb = ATTN_Q_TILE, ATTN_KV_TILE, ATTN_ROW_BLOCK
    head_lanes = [slice(h * MLA_HEAD_PAD, (h + 1) * MLA_HEAD_PAD) for h in range(N_HEADS)]
    m_ref[...] = jnp.full(m_ref.shape, -jnp.inf, F32)
    l_ref[...] = jnp.zeros(l_ref.shape, F32)
    acc_ref[...] = jnp.zeros(acc_ref.shape, F32)
    row = lax.broadcasted_iota(jnp.int32, (rb, tk), 0)
    col = lax.broadcasted_iota(jnp.int32, (rb, tk), 1)

    def step(j, shift):
        start = pl.multiple_of(j * tk, tk)
        first = 0 if shift is None else shift
        live = slice(first, tq)
        for h, lanes in enumerate(head_lanes):
            s_ref[h, live, :] = lax.dot_general(q_ref[live, lanes], k_ref[pl.ds(start, tk), lanes], NT_DIMS,
                                                preferred_element_type=F32)
            for r in range(first // rb, tq // rb):
                rows = slice(r * rb, (r + 1) * rb)
                s = s_ref[h, rows, :]
                if shift is not None and r * rb < shift + tk - 1:
                    s = jnp.where(row + r * rb >= col + shift, s, -jnp.inf)
                tiles = [s[:, c * LANE:(c + 1) * LANE] for c in range(tk // LANE)]
                m_old = m_ref[h, rows, :]
                m_new = jnp.maximum(m_old, jnp.max(functools.reduce(jnp.maximum, tiles), axis=1, keepdims=True))
                p_tiles = [jnp.exp(t - m_new) for t in tiles]
                alpha = jnp.exp(m_old - m_new)
                l_ref[h, rows, :] = alpha * l_ref[h, rows, :] + jnp.sum(functools.reduce(jnp.add, p_tiles),
                                                                      axis=1, keepdims=True)
                m_ref[h, rows, :] = m_new
                a_ref[h, rows, :] = alpha
                p_ref[h, rows, :] = jnp.concatenate([t.astype(BF16) for t in p_tiles], axis=1)
            acc_ref[h, live, :] = (a_ref[h, live, :] * acc_ref[h, live, :]
                                   + _dot(p_ref[h, live, :], v_ref[pl.ds(start, tk), lanes]))

    per_q = tq // tk

    def body(j, carry):
        step(j, None)
        return carry

    lax.fori_loop(0, i * per_q, body, 0)
    for d in range(per_q):
        step(i * per_q + d, d * tk)
    heads = [acc_ref[h] / l_ref[h] for h in range(N_HEADS)]
    lane = lax.broadcasted_iota(jnp.int32, (tq, MLA_HEAD_PAD), 1)
    lo = lane < HEAD_V
    pair = lambda a, b: jnp.where(lo, a, pltpu.roll(b, HEAD_V, 1))
    o_ref[...] = jnp.concatenate([pair(heads[0], heads[1]), pair(heads[2], heads[3])],
                                 axis=1).astype(o_ref.dtype)


def _attn_call(q, k, v, B, S):
    nq = S // ATTN_Q_TILE
    tq, tk = ATTN_Q_TILE, ATTN_KV_TILE
    stat = pltpu.VMEM((N_HEADS, tq, LANE), F32)
    return pl.pallas_call(
        _attn_kernel,
        grid=(B, nq),
        in_specs=[pl.BlockSpec((tq, MLA_W), lambda b, i: (b * nq + i, 0)),
                  pl.BlockSpec((S, MLA_W), lambda b, i: (b, 0)),
                  pl.BlockSpec((S, MLA_W), lambda b, i: (b, 0))],
        out_specs=pl.BlockSpec((tq, GROUP_WIDTH), lambda b, i: (b * nq + i, 0)),
        out_shape=jax.ShapeDtypeStruct((B * S, GROUP_WIDTH), BF16),
        scratch_shapes=[pltpu.VMEM((N_HEADS, tq, tk), F32), pltpu.VMEM((N_HEADS, tq, tk), BF16),
                        stat, stat, stat, stat],
        compiler_params=_params("parallel", "arbitrary"),
        name="mla_attention",
    )(q, k, v)


def _gelu(x):
    return 0.5 * x * (1.0 + lax.erf(x * np.float32(np.sqrt(0.5))))


def _spatial_gating(z, ln_ref, w_ref, bias_ref):
    u = _gelu(z[:, :GROUP_WIDTH])
    gv = _gelu(z[:, GROUP_WIDTH:])
    xc = gv - jnp.mean(gv, axis=-1, keepdims=True)
    v = xc * lax.rsqrt(jnp.mean(xc * xc, axis=-1, keepdims=True) + EPS) * ln_ref[...]
    C = SG_CHUNK
    causal = (lax.broadcasted_iota(jnp.int32, (C, C), 0) >= lax.broadcasted_iota(jnp.int32, (C, C), 1))
    head = lax.broadcasted_iota(jnp.int32, (1, GROUP_WIDTH), 1) // HEAD_V
    ws = [jnp.where(causal, w_ref[h], 0.0).astype(BF16) for h in range(N_HEADS)]
    bias = bias_ref[...]
    out = []
    for c in range(z.shape[0] // C):
        rows = slice(c * C, (c + 1) * C)
        vc = v[rows]
        mixed = bias
        for h in range(N_HEADS):
            mixed = mixed + _dot(ws[h], jnp.where(head == h, vc, 0.0).astype(BF16))
        out.append((u[rows] * mixed).astype(BF16))
    return jnp.concatenate(out, axis=0)


def _hg_constants():
    t = np.arange(HG_TILE)
    same = (t[:, None] // HG_GROUP) == (t[None, :] // HG_GROUP)
    m_cum = (same & (t[None, :] <= t[:, None])).astype(np.float32)
    kg = np.arange(HG_KW)
    e = np.arange(GROUP_WIDTH)
    ind = (kg[:, None] // HG_KEY == e[None, :] // HEAD_V).astype(np.float32)
    return jnp.asarray(np.tile(m_cum, (1, 3)), dtype=BF16), jnp.asarray(ind, dtype=BF16)


def _head_avg3():
    e = np.arange(GROUP_WIDTH)
    avg = (e[:, None] // HEAD_V == e[None, :] // HEAD_V).astype(np.float32) / HEAD_V
    return jnp.asarray(np.tile(avg, (3, 1)), dtype=BF16)


def _hgrn2(z, lb_ref, ng_ref, mcum3_ref, ind_ref, avg3_ref, st):
    TT, KW = HG_TILE, HG_KW
    n_grp, n_sub = TT // HG_GROUP, HG_GROUP // HG_SUB
    q = z[:, :KW]
    f = z[:, KW:2 * KW]
    vi = z[:, 2 * KW:2 * KW + GROUP_WIDTH]
    g = z[:, 2 * KW + GROUP_WIDTH:]
    lb = lb_ref[...]
    forget = lb + (1.0 - lb) * jax.nn.sigmoid(f)
    kk = 1.0 - forget
    b = _sum_rows_f32(mcum3_ref, jnp.log(forget))
    head_v = lax.broadcasted_iota(jnp.int32, (1, GROUP_WIDTH), 1) // HEAD_V
    ind = ind_ref[...]

    tiles = lambda a: a.reshape(TT // HG_SUB, HG_SUB, a.shape[-1])
    q3, f3, v3 = tiles(q), tiles(forget), tiles(vi)
    r_sub = lax.broadcasted_iota(jnp.int32, (1, HG_SUB, 1), 1)
    o = _dot((q * kk).astype(BF16), ind) * vi
    decay, f_prev = None, f3
    for d in range(1, HG_SUB):
        f_d = pltpu.roll(f3, d, 1)
        decay = f3 if d == 1 else decay * f_prev
        x = jnp.where(r_sub >= d, q3 * (1.0 - f_d) * decay, 0.0)
        w = _dot(x.reshape(TT, KW).astype(BF16), ind)
        o = o + w * pltpu.roll(v3, d, 1).reshape(TT, GROUP_WIDTH)
        f_prev = f_d

    groups = lambda a: a.reshape(n_grp, n_sub, HG_SUB, a.shape[-1])
    b4, kk4, q4 = groups(b), groups(kk), groups(q)
    ends = jnp.broadcast_to(b4[:, :, HG_SUB - 1:HG_SUB, :], b4.shape)
    bsub4 = jnp.concatenate([jnp.zeros_like(ends[:, :1]), ends[:, :n_sub - 1]], axis=1)
    blast4 = ends[:, n_sub - 1:]

    qp4 = q4 * jnp.exp(b4 - bsub4)
    zeros4 = lambda n: [jnp.zeros((n_grp, n, HG_SUB, KW), F32)] if n else []
    q_slots, k_slots = [], []
    for i in range(1, n_sub):
        k_i = kk4[:, :i] * jnp.exp(bsub4[:, i:i + 1] - b4[:, :i])
        k_slots.append(jnp.concatenate([k_i] + zeros4(n_sub - i), axis=1).reshape(TT, KW))
        q_slots.append(jnp.concatenate(zeros4(i) + [qp4[:, i:i + 1]] + zeros4(n_sub - 1 - i),
                                       axis=1).reshape(TT, KW))
    same_group = ((lax.broadcasted_iota(jnp.int32, (TT, TT), 0) // HG_GROUP)
                  == (lax.broadcasted_iota(jnp.int32, (TT, TT), 1) // HG_GROUP))
    for h in range(N_HEADS):
        lanes = slice(h * HG_KEY, (h + 1) * HG_KEY)
        lhs = jnp.concatenate([s[:, lanes] for s in q_slots], axis=1).astype(BF16)
        rhs = jnp.concatenate([s[:, lanes] for s in k_slots], axis=1).astype(BF16)
        a = lax.dot_general(lhs, rhs, NT_DIMS, preferred_element_type=F32)
        a = jnp.where(same_group, a, 0.0).astype(BF16)
        o = o + _dot(a, jnp.where(head_v == h, vi, 0.0).astype(BF16))

    qe = (q * jnp.exp(b)).astype(BF16)
    k_dec = (kk4 * jnp.exp(blast4 - b4)).reshape(TT, KW).astype(BF16)
    e_last = jnp.exp(blast4[:, 0, 0:1, :])
    vb = vi.astype(BF16)
    same_head = (lax.broadcasted_iota(jnp.int32, (GROUP_WIDTH, 1), 0) // HEAD_V
                 == lax.broadcasted_iota(jnp.int32, (1, KW), 1) // HG_KEY)
    inter = []
    for gi in range(n_grp):
        rows = slice(gi * HG_GROUP, (gi + 1) * HG_GROUP)
        inter.append(lax.dot_general(qe[rows], st.astype(BF16), NT_DIMS, preferred_element_type=F32))
        upd = lax.dot_general(vb[rows], k_dec[rows], TN_DIMS, preferred_element_type=F32)
        st = st * e_last[gi] + jnp.where(same_head, upd, 0.0)
    o = o + jnp.concatenate(inter, axis=0)

    ms = _sum_lanes_f32(o * o, avg3_ref)
    return (o * lax.rsqrt(ms + EPS) * ng_ref[...] * _silu(g)).astype(BF16), st


def _ret_constants():
    C = RET_CHUNK
    lg = np.log1p(-np.power(2.0, -(5.0 + np.arange(N_HEADS, dtype=np.float64))))
    pos = np.arange(C, dtype=np.float64)
    rel = pos[:, None] - pos[None, :]
    decay = np.where(rel >= 0, np.exp(np.maximum(rel, 0.0)[None] * lg[:, None, None]), 0.0)
    lane = np.arange(GROUP_WIDTH)
    lg_qk = lg[(lane % 128) // (RET_QK // 2)]
    lg_v = lg[lane // HEAD_V]
    gq = np.exp((pos + 1.0)[:, None] * lg_qk[None, :])
    gk = np.exp((C - 1.0 - pos)[:, None] * lg_qk[None, :])
    gc = np.exp(C * lg_v)[None, :]
    f = lambda a: jnp.asarray(a.astype(np.float32))
    return f(decay), f(gq), f(gk), f(gc)


def _retention(z, c, s, dec_ref, gq_ref, gk_ref, gc_ref, ng_ref, avg3_ref, st):
    W = GROUP_WIDTH
    q1, q2 = z[:, 0:128], z[:, 128:256]
    k1, k2 = z[:, 256:384], z[:, 384:512]
    v = z[:, 2 * W:3 * W]
    g = z[:, 3 * W:]
    qr = jnp.concatenate([q1 * c - q2 * s, q1 * s + q2 * c], axis=1)
    kr = jnp.concatenate([k1 * c - k2 * s, k1 * s + k2 * c], axis=1) * (RET_QK ** -0.5)
    lane = lax.broadcasted_iota(jnp.int32, (1, W), 1)
    head_qk = (lane & 127) // (RET_QK // 2)
    head_v = lane // HEAD_V
    krb = kr.astype(BF16)
    vb = v.astype(BF16)

    o = _dot((qr * gq_ref[...]).astype(BF16), st.astype(BF16))
    for h in range(N_HEADS):
        qm = jnp.where(head_qk == h, qr, 0.0).astype(BF16)
        sc = lax.dot_general(qm, krb, NT_DIMS, preferred_element_type=F32) * dec_ref[h]
        o = o + _dot(sc.astype(BF16), jnp.where(head_v == h, vb, jnp.zeros_like(vb)))

    upd = lax.dot_general((kr * gk_ref[...]).astype(BF16), vb, TN_DIMS, preferred_element_type=F32)
    row_head = (lax.broadcasted_iota(jnp.int32, (W, 1), 0) & 127) // (RET_QK // 2)
    st = st * gc_ref[...] + jnp.where(row_head == head_v, upd, 0.0)

    oc = o - _sum_lanes_f32(o, avg3_ref)
    var = _sum_lanes_f32(oc * oc, avg3_ref)
    return (oc * lax.rsqrt(var + EPS) * ng_ref[...] * _silu(g)).astype(BF16), st


def _mix_kernel(tiles_per_seq,
                h_ref, cb_ref, su_ref, sd_ref, cos_ref, sin_ref,
                g1_ref, wg_ref, wu_ref, wd_ref, g2_ref, win_ref,
                qn_ref, kvn_ref, wuq_ref, wukv_ref,
                sgln_ref, sgw_ref, sgb_ref,
                lb_ref, hgn_ref, mcum3_ref, ind_ref, avg3_ref,
                dec_ref, gq_ref, gk_ref, gc_ref, retn_ref,
                h_out, q_out, k_out, v_out, yb_out, yc_out, yd_out,
                hg_st, ret_st):
    @pl.when(pl.program_id(0) % tiles_per_seq == 0)
    def _():
        hg_st[...] = jnp.zeros_like(hg_st)
        ret_st[...] = jnp.zeros_like(ret_st)

    h1 = _ffn_half_step(h_ref[...], g1_ref, wg_ref, wu_ref, wd_ref)
    h_out[...] = h1
    x = _rms(h1, g2_ref[...]).astype(BF16)
    zs = []
    off = 0
    for w in (ZM_W, ZS_W, ZH_W, ZR_W):
        zs.append(_dot(x, win_ref[:, off:off + w]))
        off += w
    zm, zg, zh, zr = zs

    q, k, v = _mla_prep(zm, qn_ref, kvn_ref, wuq_ref, wukv_ref, cb_ref[...], su_ref[...], sd_ref[...])
    q_out[...] = q
    k_out[...] = k
    v_out[...] = v
    yb_out[...] = _spatial_gating(zg, sgln_ref, sgw_ref, sgb_ref)

    st = hg_st[...]
    for n in range(TOKEN_TILE // HG_TILE):
        rows = slice(n * HG_TILE, (n + 1) * HG_TILE)
        y, st = _hgrn2(zh[rows], lb_ref, hgn_ref, mcum3_ref, ind_ref, avg3_ref, st)
        yc_out[rows, :] = y
    hg_st[...] = st

    st = ret_st[...]
    cos, sin = cos_ref[...], sin_ref[...]
    for n in range(TOKEN_TILE // RET_CHUNK):
        rows = slice(n * RET_CHUNK, (n + 1) * RET_CHUNK)
        y, st = _retention(zr[rows], cos[rows], sin[rows], dec_ref, gq_ref, gk_ref, gc_ref, retn_ref,
                           avg3_ref, st)
        yd_out[rows, :] = y
    ret_st[...] = st


def _mix_call(h, tables, weights, S):
    T = h.shape[0]
    tm = TOKEN_TILE
    tok = lambda w: pl.BlockSpec((tm, w), lambda i: (i, 0))
    out_w = (D_MODEL, MLA_W, MLA_W, MLA_W, GROUP_WIDTH, GROUP_WIDTH, GROUP_WIDTH)
    out_t = (F32,) + (BF16,) * 6
    return pl.pallas_call(
        functools.partial(_mix_kernel, S // tm),
        grid=(T // tm,),
        in_specs=[tok(D_MODEL)] + [tok(LANE)] * 5 + [_weight_spec(w) for w in weights],
        out_specs=[tok(w) for w in out_w],
        out_shape=[jax.ShapeDtypeStruct((T, w), t) for w, t in zip(out_w, out_t)],
        scratch_shapes=[pltpu.VMEM((GROUP_WIDTH, HG_KW), F32), pltpu.VMEM((GROUP_WIDTH, GROUP_WIDTH), F32)],
        compiler_params=_params("arbitrary"),
        name="ffn_inproj_mixers",
    )(h, *tables, *[_weight_arg(w) for w in weights])


def _post_kernel(final, h_ref, ya_ref, yb_ref, yc_ref, yd_ref, p_ref, wo_ref, g_ref, wg_ref, wu_ref,
                 wd_ref, gp_ref, wpp_ref, wpg_ref, gf_ref, out_ref):
    h = h_ref[...]
    for n, y_ref in enumerate((ya_ref, yb_ref, yc_ref, yd_ref)):
        h = h + _dot(y_ref[...], wo_ref[n * GROUP_WIDTH:(n + 1) * GROUP_WIDTH, :])
    h = _ffn_half_step(h, g_ref, wg_ref, wu_ref, wd_ref)
    gate = jax.nn.sigmoid(_dot(_rms(h, gp_ref[...]).astype(BF16), wpg_ref[...]))
    h = h + _dot(p_ref[...].astype(BF16), wpp_ref[...]) * gate
    if final:
        h = _rms(h, gf_ref[...])
    out_ref[...] = h


def _post_call(h, ys, p, weights, final):
    T = h.shape[0]
    tm = TOKEN_TILE
    tok = lambda w: pl.BlockSpec((tm, w), lambda i: (i, 0))
    p_spec = pl.BlockSpec((None, tm, PLE_DIM), lambda i: (p.index, i, 0))
    return pl.pallas_call(
        functools.partial(_post_kernel, final),
        grid=(T // tm,),
        in_specs=[tok(D_MODEL)] + [tok(GROUP_WIDTH)] * 4 + [p_spec] + [_weight_spec(w) for w in weights],
        out_specs=tok(D_MODEL),
        out_shape=jax.ShapeDtypeStruct((T, D_MODEL), F32),
        compiler_params=_params("parallel"),
        name="post_outproj_ffn_ple",
    )(h, *ys, p.stacked, *[_weight_arg(w) for w in weights])


def _prep_w_in(w):
    w = w.astype(BF16)
    lat = MLA_Q_LORA + MLA_KV_LORA
    pe = jnp.pad(w[..., lat:lat + MLA_ROPE], ((0, 0), (0, 0), (MLA_NOPE, MLA_HEAD_PAD - MLA_NOPE - MLA_ROPE)))
    ret = lat + MLA_ROPE + ZS_W + ZH_W
    half = RET_QK // 2
    src = np.arange(N_HEADS * RET_QK)
    dst = (src % RET_QK) // half * (N_HEADS * half) + src // RET_QK * half + src % half
    perm = np.zeros((src.size, src.size), np.float32)
    perm[src, dst] = 1.0
    perm = jnp.asarray(perm, dtype=BF16)
    regroup = lambda m: jnp.einsum('ldc,ce->lde', m, perm, preferred_element_type=BF16)
    rq, rk = w[..., ret:ret + 256], w[..., ret + 256:ret + 512]
    return jnp.concatenate([w[..., :lat], pe, w[..., lat + MLA_ROPE:ret], regroup(rq), regroup(rk),
                            w[..., ret + 512:]], axis=-1)


def _prep_w_uq(w):
    w = w.reshape(MLA_Q_LORA, N_HEADS, MLA_NOPE + MLA_ROPE)
    w = jnp.pad(w, ((0, 0), (0, 0), (0, MLA_HEAD_PAD - MLA_NOPE - MLA_ROPE)))
    return w.reshape(MLA_Q_LORA, MLA_W).astype(BF16)


def _prep_w_ukv(w):
    w = w.reshape(MLA_KV_LORA, N_HEADS, MLA_NOPE + HEAD_V)
    pad = lambda m: jnp.pad(m, ((0, 0), (0, 0), (0, MLA_HEAD_PAD - m.shape[-1]))).reshape(MLA_KV_LORA, -1)
    return jnp.concatenate([pad(w[..., :MLA_NOPE]), pad(w[..., MLA_NOPE:])], axis=1).astype(BF16)


def _rope_tables(positions):
    pos = positions.astype(F32).reshape(-1)[:, None]

    def inv_freq(d):
        return ROPE_THETA ** (-jnp.arange(0, d, 2, dtype=F32) / d)

    half = MLA_ROPE // 2
    lane = np.arange(MLA_HEAD_PAD)
    x1 = ((lane >= MLA_NOPE) & (lane < MLA_NOPE + half)).astype(np.float32)
    x2 = ((lane >= MLA_NOPE + half) & (lane < MLA_NOPE + MLA_ROPE)).astype(np.float32)
    nope = (lane < MLA_NOPE).astype(np.float32)
    freq = jnp.tile(inv_freq(RET_QK), N_HEADS)
    freq = freq.at[MLA_NOPE:MLA_NOPE + half].set(inv_freq(MLA_ROPE))
    freq = freq.at[MLA_NOPE + half:MLA_NOPE + MLA_ROPE].set(inv_freq(MLA_ROPE))
    ang = pos * freq[None, :]
    c, s = jnp.cos(ang), jnp.sin(ang)
    cb = c * (x1 + x2)[None, :] + nope[None, :]
    su = s * x2[None, :]
    sd = -s * x1[None, :]
    in_mla = jnp.asarray((x1 + x2) > 0)[None, :]
    cos_r = jnp.where(in_mla, jnp.roll(c, RET_QK // 2, axis=1), c)
    sin_r = jnp.where(in_mla, jnp.roll(s, RET_QK // 2, axis=1), s)
    return cb, su, sd, cos_r, sin_r


def kernel(x, p, positions, ffn1_norm, ffn1_w_gate, ffn1_w_up, ffn1_w_down, mix_norm, w_in, mla_q_norm, mla_w_uq, mla_kv_norm, mla_w_ukv, sg_ln, sg_w_s, sg_b_s, hg_lb_logits, hg_norm, ret_norm, w_out, ffn2_norm, ffn2_w_gate, ffn2_w_up, ffn2_w_down, ple_norm, ple_w_proj, ple_w_gate, final_norm):
    B, S, D = x.shape
    L = w_in.shape[0]
    T = B * S
    row = lambda a: a.reshape(1, -1).astype(F32)

    tables = _rope_tables(positions)
    mcum3, ind = _hg_constants()
    dec, gq, gk, gc = _ret_constants()
    avg3 = _head_avg3()
    lb_all = jnp.cumsum(jax.nn.softmax(hg_lb_logits.astype(F32), axis=0), axis=0)
    lb_all = lb_all - lb_all[0:1]
    gf = row(final_norm)
    bf16 = lambda a: a.astype(BF16)
    ffn1, ffn2 = [tuple(bf16(w) for w in ws) for ws in ((ffn1_w_gate, ffn1_w_up, ffn1_w_down),
                                                       (ffn2_w_gate, ffn2_w_up, ffn2_w_down))]
    win_all, wo_all, wpp_all, wpg_all = _prep_w_in(w_in), bf16(w_out), bf16(ple_w_proj), bf16(ple_w_gate)
    p_all = p.reshape(L, T, PLE_DIM)

    h = x.reshape(T, D)
    for l in range(L):
        sg_bias = jnp.repeat(sg_b_s[l].T, HEAD_V, axis=1)
        weights = (row(ffn1_norm[l]), *[_Layer(w, l) for w in ffn1],
                   row(mix_norm[l]), _Layer(win_all, l),
                   row(mla_q_norm[l]), row(mla_kv_norm[l]), _prep_w_uq(mla_w_uq[l]), _prep_w_ukv(mla_w_ukv[l]),
                   row(sg_ln[l]), sg_w_s[l], sg_bias,
                   row(lb_all[l]), row(hg_norm[l]), mcum3, ind, avg3,
                   dec, gq, gk, gc, row(ret_norm[l]))
        h, q, k, v, y_b, y_c, y_d = _mix_call(h, tables, weights, S)
        y_a = _attn_call(q, k, v, B, S)
        weights = (_Layer(wo_all, l), row(ffn2_norm[l]), *[_Layer(w, l) for w in ffn2],
                   row(ple_norm[l]), _Layer(wpp_all, l), _Layer(wpg_all, l), gf)
        h = _post_call(h, (y_a, y_b, y_c, y_d), _Layer(p_all, l), weights, final=(l == L - 1))
    return h.reshape(B, S, D)
```

```python
import functools
from typing import NamedTuple

import numpy as np
import jax
import jax.numpy as jnp
from jax import lax
from jax.experimental import pallas as pl
from jax.experimental.pallas import tpu as pltpu

F32 = jnp.float32
BF16 = jnp.bfloat16
EPS = 1e-6
ROPE_THETA = 10000.0

LANE = 128
D_MODEL = 1024
D_FF = 2816
FF_CHUNK = 256
N_FF_CHUNKS = D_FF // FF_CHUNK
PLE_DIM = 256
GROUP_WIDTH = 256
N_HEADS = 4
HEAD_V = GROUP_WIDTH // N_HEADS

MLA_NOPE, MLA_ROPE, MLA_Q_LORA, MLA_KV_LORA = 64, 32, 256, 128
MLA_HEAD_PAD = 128
MLA_W = N_HEADS * MLA_HEAD_PAD
SG_CHUNK = 128
HG_KEY = 128
HG_KW = N_HEADS * HG_KEY
HG_SUB = 8
HG_GROUP = 64
HG_TILE = 256
RET_QK = 64
RET_CHUNK = 256

ZM_W, ZS_W, ZH_W, ZR_W = 512, 512, 1536, 1024

TOKEN_TILE = 512
ATTN_Q_TILE = 1024
ATTN_KV_TILE = 512
ATTN_ROW_BLOCK = 32
VMEM_LIMIT = 60 * 1024 * 1024

NT_DIMS = (((1,), (1,)), ((), ()))
TN_DIMS = (((0,), (0,)), ((), ()))


def _rms(x, g):
    return x * lax.rsqrt(jnp.mean(x * x, axis=-1, keepdims=True) + EPS) * g


def _silu(x):
    return x * jax.nn.sigmoid(x)


def _dot(a, b):
    return jnp.dot(a, b, preferred_element_type=F32)


def _params(*sem):
    return pltpu.CompilerParams(dimension_semantics=sem, vmem_limit_bytes=VMEM_LIMIT)


def _full(shape):
    return pl.BlockSpec(shape, lambda *_: (0,) * len(shape))


class _Layer(NamedTuple):
    stacked: jax.Array
    index: int


def _weight_spec(w):
    if isinstance(w, _Layer):
        shape = w.stacked.shape[1:]
        return pl.BlockSpec((None,) + shape, lambda *_: (w.index,) + (0,) * len(shape),
                            pipeline_mode=pl.Buffered(1))
    return _full(w.shape)


def _weight_arg(w):
    return w.stacked if isinstance(w, _Layer) else w


def _split3(x):
    p1 = x.astype(BF16)
    r = x - p1.astype(F32)
    p2 = r.astype(BF16)
    p3 = (r - p2.astype(F32)).astype(BF16)
    return p1, p2, p3


def _sum_rows_f32(m3_ref, x):
    return _dot(m3_ref[...], jnp.concatenate(_split3(x), axis=0))


def _sum_lanes_f32(x, m3_ref):
    return _dot(jnp.concatenate(_split3(x), axis=1), m3_ref[...])


def _ffn_half_step(h, g_ref, wg_ref, wu_ref, wd_ref):
    x = _rms(h, g_ref[...]).astype(BF16)
    acc = jnp.zeros_like(h)
    for c in range(N_FF_CHUNKS):
        cols = slice(c * FF_CHUNK, (c + 1) * FF_CHUNK)
        a = (_silu(_dot(x, wg_ref[:, cols])) * _dot(x, wu_ref[:, cols])).astype(BF16)
        acc = acc + _dot(a, wd_ref[cols, :])
    return h + 0.5 * acc


def _rope_lanes(x, cb, su, sd):
    w = x.shape[1]
    half = MLA_ROPE // 2
    return x * cb + pltpu.roll(x, half, 1) * su + pltpu.roll(x, w - half, 1) * sd


def _mla_prep(z, qn_ref, kvn_ref, wuq_ref, wukv_ref, cb, su, sd):
    c_q = z[:, :MLA_Q_LORA]
    c_kv = z[:, MLA_Q_LORA:MLA_Q_LORA + MLA_KV_LORA]
    k_pe = z[:, MLA_Q_LORA + MLA_KV_LORA:]
    q = _dot(_rms(c_q, qn_ref[...]).astype(BF16), wuq_ref[...])
    kv = _dot(_rms(c_kv, kvn_ref[...]).astype(BF16), wukv_ref[...])
    tile4 = lambda t: jnp.concatenate([t] * N_HEADS, axis=1)
    scale = (MLA_NOPE + MLA_ROPE) ** -0.5
    q = (_rope_lanes(q, tile4(cb), tile4(su), tile4(sd)) * scale).astype(BF16)
    k = (kv[:, :MLA_W] + tile4(_rope_lanes(k_pe, cb, su, sd))).astype(BF16)
    return q, k, kv[:, MLA_W:].astype(BF16)


def _attn_kernel(q_ref, k_ref, v_ref, o_ref, s_ref, p_ref, m_ref, l_ref, a_ref, acc_ref):
    i = pl.program_id(1)
    tq, tk, rb = ATTN_Q_TILE, ATTN_KV_TILE, ATTN_ROW_BLOCK
    head_lanes = [slice(h * MLA_HEAD_PAD, (h + 1) * MLA_HEAD_PAD) for h in range(N_HEADS)]
    m_ref[...] = jnp.full(m_ref.shape, -jnp.inf, F32)
    l_ref[...] = jnp.zeros(l_ref.shape, F32)
    acc_ref[...] = jnp.zeros(acc_ref.shape, F32)
    row = lax.broadcasted_iota(jnp.int32, (rb, tk), 0)
    col = lax.broadcasted_iota(jnp.int32, (rb, tk), 1)

    def step(j, shift):
        start = pl.multiple_of(j * tk, tk)
        first = 0 if shift is None else shift
        live = slice(first, tq)
        for h, lanes in enumerate(head_lanes):
            s_ref[h, live, :] = lax.dot_general(q_ref[live, lanes], k_ref[pl.ds(start, tk), lanes], NT_DIMS,
                                                preferred_element_type=F32)
            for r in range(first // rb, tq // rb):
                rows = slice(r * rb, (r + 1) * rb)
                s = s_ref[h, rows, :]
                if shift is not None and r * rb < shift + tk - 1:
                    s = jnp.where(row + r * rb >= col + shift, s, -jnp.inf)
                tiles = [s[:, c * LANE:(c + 1) * LANE] for c in range(tk // LANE)]
                m_old = m_ref[h, rows, :]
                m_new = jnp.maximum(m_old, jnp.max(functools.reduce(jnp.maximum, tiles), axis=1, keepdims=True))
                p_tiles = [jnp.exp(t - m_new) for t in tiles]
                alpha = jnp.exp(m_old - m_new)
                l_ref[h, rows, :] = alpha * l_ref[h, rows, :] + jnp.sum(functools.reduce(jnp.add, p_tiles),
                                                                      axis=1, keepdims=True)
                m_ref[h, rows, :] = m_new
                a_ref[h, rows, :] = alpha
                p_ref[h, rows, :] = jnp.concatenate([t.astype(BF16) for t in p_tiles], axis=1)
            acc_ref[h, live, :] = (a_ref[h, live, :] * acc_ref[h, live, :]
                                   + _dot(p_ref[h, live, :], v_ref[pl.ds(start, tk), lanes]))

    per_q = tq // tk

    def body(j, carry):
        step(j, None)
        return carry

    lax.fori_loop(0, i * per_q, body, 0)
    for d in range(per_q):
        step(i * per_q + d, d * tk)
    heads = [acc_ref[h] / l_ref[h] for h in range(N_HEADS)]
    lane = lax.broadcasted_iota(jnp.int32, (tq, MLA_HEAD_PAD), 1)
    lo = lane < HEAD_V
    pair = lambda a, b: jnp.where(lo, a, pltpu.roll(b, HEAD_V, 1))
    o_ref[...] = jnp.concatenate([pair(heads[0], heads[1]), pair(heads[2], heads[3])],
                                 axis=1).astype(o_ref.dtype)


def _attn_call(q, k, v, B, S):
    nq = S // ATTN_Q_TILE
    tq, tk = ATTN_Q_TILE, ATTN_KV_TILE
    stat = pltpu.VMEM((N_HEADS, tq, LANE), F32)
    return pl.pallas_call(
        _attn_kernel,
        grid=(B, nq),
        in_specs=[pl.BlockSpec((tq, MLA_W), lambda b, i: (b * nq + i, 0)),
                  pl.BlockSpec((S, MLA_W), lambda b, i: (b, 0)),
                  pl.BlockSpec((S, MLA_W), lambda b, i: (b, 0))],
        out_specs=pl.BlockSpec((tq, GROUP_WIDTH), lambda b, i: (b * nq + i, 0)),
        out_shape=jax.ShapeDtypeStruct((B * S, GROUP_WIDTH), BF16),
        scratch_shapes=[pltpu.VMEM((N_HEADS, tq, tk), F32), pltpu.VMEM((N_HEADS, tq, tk), BF16),
                        stat, stat, stat, stat],
        compiler_params=_params("parallel", "arbitrary"),
        name="mla_attention",
    )(q, k, v)


def _gelu(x):
    return 0.5 * x * (1.0 + lax.erf(x * np.float32(np.sqrt(0.5))))


def _spatial_gating(z, ln_ref, w_ref, bias_ref):
    u = _gelu(z[:, :GROUP_WIDTH])
    gv = _gelu(z[:, GROUP_WIDTH:])
    xc = gv - jnp.mean(gv, axis=-1, keepdims=True)
    v = xc * lax.rsqrt(jnp.mean(xc * xc, axis=-1, keepdims=True) + EPS) * ln_ref[...]
    C = SG_CHUNK
    causal = (lax.broadcasted_iota(jnp.int32, (C, C), 0) >= lax.broadcasted_iota(jnp.int32, (C, C), 1))
    head = lax.broadcasted_iota(jnp.int32, (1, GROUP_WIDTH), 1) // HEAD_V
    ws = [jnp.where(causal, w_ref[h], 0.0).astype(BF16) for h in range(N_HEADS)]
    bias = bias_ref[...]
    out = []
    for c in range(z.shape[0] // C):
        rows = slice(c * C, (c + 1) * C)
        vc = v[rows]
        mixed = bias
        for h in range(N_HEADS):
            mixed = mixed + _dot(ws[h], jnp.where(head == h, vc, 0.0).astype(BF16))
        out.append((u[rows] * mixed).astype(BF16))
    return jnp.concatenate(out, axis=0)


def _hg_constants():
    t = np.arange(HG_TILE)
    same = (t[:, None] // HG_GROUP) == (t[None, :] // HG_GROUP)
    m_cum = (same & (t[None, :] <= t[:, None])).astype(np.float32)
    kg = np.arange(HG_KW)
    e = np.arange(GROUP_WIDTH)
    ind = (kg[:, None] // HG_KEY == e[None, :] // HEAD_V).astype(np.float32)
    return jnp.asarray(np.tile(m_cum, (1, 3)), dtype=BF16), jnp.asarray(ind, dtype=BF16)


def _head_avg3():
    e = np.arange(GROUP_WIDTH)
    avg = (e[:, None] // HEAD_V == e[None, :] // HEAD_V).astype(np.float32) / HEAD_V
    return jnp.asarray(np.tile(avg, (3, 1)), dtype=BF16)


def _hgrn2(z, lb_ref, ng_ref, mcum3_ref, ind_ref, avg3_ref, st):
    TT, KW = HG_TILE, HG_KW
    n_grp, n_sub = TT // HG_GROUP, HG_GROUP // HG_SUB
    q = z[:, :KW]
    f = z[:, KW:2 * KW]
    vi = z[:, 2 * KW:2 * KW + GROUP_WIDTH]
    g = z[:, 2 * KW + GROUP_WIDTH:]
    lb = lb_ref[...]
    forget = lb + (1.0 - lb) * jax.nn.sigmoid(f)
    kk = 1.0 - forget
    b = _sum_rows_f32(mcum3_ref, jnp.log(forget))
    head_v = lax.broadcasted_iota(jnp.int32, (1, GROUP_WIDTH), 1) // HEAD_V
    ind = ind_ref[...]

    tiles = lambda a: a.reshape(TT // HG_SUB, HG_SUB, a.shape[-1])
    q3, f3, v3 = tiles(q), tiles(forget), tiles(vi)
    r_sub = lax.broadcasted_iota(jnp.int32, (1, HG_SUB, 1), 1)
    o = _dot((q * kk).astype(BF16), ind) * vi
    decay, f_prev = None, f3
    for d in range(1, HG_SUB):
        f_d = pltpu.roll(f3, d, 1)
        decay = f3 if d == 1 else decay * f_prev
        x = jnp.where(r_sub >= d, q3 * (1.0 - f_d) * decay, 0.0)
        w = _dot(x.reshape(TT, KW).astype(BF16), ind)
        o = o + w * pltpu.roll(v3, d, 1).reshape(TT, GROUP_WIDTH)
        f_prev = f_d

    groups = lambda a: a.reshape(n_grp, n_sub, HG_SUB, a.shape[-1])
    b4, kk4, q4 = groups(b), groups(kk), groups(q)
    ends = jnp.broadcast_to(b4[:, :, HG_SUB - 1:HG_SUB, :], b4.shape)
    bsub4 = jnp.concatenate([jnp.zeros_like(ends[:, :1]), ends[:, :n_sub - 1]], axis=1)
    blast4 = ends[:, n_sub - 1:]

    qp4 = q4 * jnp.exp(b4 - bsub4)
    zeros4 = lambda n: [jnp.zeros((n_grp, n, HG_SUB, KW), F32)] if n else []
    q_slots, k_slots = [], []
    for i in range(1, n_sub):
        k_i = kk4[:, :i] * jnp.exp(bsub4[:, i:i + 1] - b4[:, :i])
        k_slots.append(jnp.concatenate([k_i] + zeros4(n_sub - i), axis=1).reshape(TT, KW))
        q_slots.append(jnp.concatenate(zeros4(i) + [qp4[:, i:i + 1]] + zeros4(n_sub - 1 - i),
                                       axis=1).reshape(TT, KW))
    same_group = ((lax.broadcasted_iota(jnp.int32, (TT, TT), 0) // HG_GROUP)
                  == (lax.broadcasted_iota(jnp.int32, (TT, TT), 1) // HG_GROUP))
    for h in range(N_HEADS):
        lanes = slice(h * HG_KEY, (h + 1) * HG_KEY)
        lhs = jnp.concatenate([s[:, lanes] for s in q_slots], axis=1).astype(BF16)
        rhs = jnp.concatenate([s[:, lanes] for s in k_slots], axis=1).astype(BF16)
        a = lax.dot_general(lhs, rhs, NT_DIMS, preferred_element_type=F32)
        a = jnp.where(same_group, a, 0.0).astype(BF16)
        o = o + _dot(a, jnp.where(head_v == h, vi, 0.0).astype(BF16))

    qe = (q * jnp.exp(b)).astype(BF16)
    k_dec = (kk4 * jnp.exp(blast4 - b4)).reshape(TT, KW).astype(BF16)
    e_last = jnp.exp(blast4[:, 0, 0:1, :])
    vb = vi.astype(BF16)
    same_head = (lax.broadcasted_iota(jnp.int32, (GROUP_WIDTH, 1), 0) // HEAD_V
                 == lax.broadcasted_iota(jnp.int32, (1, KW), 1) // HG_KEY)
    inter = []
    for gi in range(n_grp):
        rows = slice(gi * HG_GROUP, (gi + 1) * HG_GROUP)
        inter.append(lax.dot_general(qe[rows], st.astype(BF16), NT_DIMS, preferred_element_type=F32))
        upd = lax.dot_general(vb[rows], k_dec[rows], TN_DIMS, preferred_element_type=F32)
        st = st * e_last[gi] + jnp.where(same_head, upd, 0.0)
    o = o + jnp.concatenate(inter, axis=0)

    ms = _sum_lanes_f32(o * o, avg3_ref)
    return (o * lax.rsqrt(ms + EPS) * ng_ref[...] * _silu(g)).astype(BF16), st


def _ret_constants():
    C = RET_CHUNK
    lg = np.log1p(-np.power(2.0, -(5.0 + np.arange(N_HEADS, dtype=np.float64))))
    pos = np.arange(C, dtype=np.float64)
    rel = pos[:, None] - pos[None, :]
    decay = np.where(rel >= 0, np.exp(np.maximum(rel, 0.0)[None] * lg[:, None, None]), 0.0)
    lane = np.arange(GROUP_WIDTH)
    lg_qk = lg[(lane % 128) // (RET_QK // 2)]
    lg_v = lg[lane // HEAD_V]
    gq = np.exp((pos + 1.0)[:, None] * lg_qk[None, :])
    gk = np.exp((C - 1.0 - pos)[:, None] * lg_qk[None, :])
    gc = np.exp(C * lg_v)[None, :]
    f = lambda a: jnp.asarray(a.astype(np.float32))
    return f(decay), f(gq), f(gk), f(gc)


def _retention(z, c, s, dec_ref, gq_ref, gk_ref, gc_ref, ng_ref, avg3_ref, st):
    W = GROUP_WIDTH
    q1, q2 = z[:, 0:128], z[:, 128:256]
    k1, k2 = z[:, 256:384], z[:, 384:512]
    v = z[:, 2 * W:3 * W]
    g = z[:, 3 * W:]
    qr = jnp.concatenate([q1 * c - q2 * s, q1 * s + q2 * c], axis=1)
    kr = jnp.concatenate([k1 * c - k2 * s, k1 * s + k2 * c], axis=1) * (RET_QK ** -0.5)
    lane = lax.broadcasted_iota(jnp.int32, (1, W), 1)
    head_qk = (lane & 127) // (RET_QK // 2)
    head_v = lane // HEAD_V
    krb = kr.astype(BF16)
    vb = v.astype(BF16)

    o = _dot((qr * gq_ref[...]).astype(BF16), st.astype(BF16))
    for h in range(N_HEADS):
        qm = jnp.where(head_qk == h, qr, 0.0).astype(BF16)
        sc = lax.dot_general(qm, krb, NT_DIMS, preferred_element_type=F32) * dec_ref[h]
        o = o + _dot(sc.astype(BF16), jnp.where(head_v == h, vb, jnp.zeros_like(vb)))

    upd = lax.dot_general((kr * gk_ref[...]).astype(BF16), vb, TN_DIMS, preferred_element_type=F32)
    row_head = (lax.broadcasted_iota(jnp.int32, (W, 1), 0) & 127) // (RET_QK // 2)
    st = st * gc_ref[...] + jnp.where(row_head == head_v, upd, 0.0)

    oc = o - _sum_lanes_f32(o, avg3_ref)
    var = _sum_lanes_f32(oc * oc, avg3_ref)
    return (oc * lax.rsqrt(var + EPS) * ng_ref[...] * _silu(g)).astype(BF16), st


def _mix_kernel(tiles_per_seq,
                h_ref, cb_ref, su_ref, sd_ref, cos_ref, sin_ref,
                g1_ref, wg_ref, wu_ref, wd_ref, g2_ref, win_ref,
                qn_ref, kvn_ref, wuq_ref, wukv_ref,
                sgln_ref, sgw_ref, sgb_ref,
                lb_ref, hgn_ref, mcum3_ref, ind_ref, avg3_ref,
                dec_ref, gq_ref, gk_ref, gc_ref, retn_ref,
                h_out, q_out, k_out, v_out, yb_out, yc_out, yd_out,
                hg_st, ret_st):
    @pl.when(pl.program_id(0) % tiles_per_seq == 0)
    def _():
        hg_st[...] = jnp.zeros_like(hg_st)
        ret_st[...] = jnp.zeros_like(ret_st)

    h1 = _ffn_half_step(h_ref[...], g1_ref, wg_ref, wu_ref, wd_ref)
    h_out[...] = h1
    x = _rms(h1, g2_ref[...]).astype(BF16)
    zs = []
    off = 0
    for w in (ZM_W, ZS_W, ZH_W, ZR_W):
        zs.append(_dot(x, win_ref[:, off:off + w]))
        off += w
    zm, zg, zh, zr = zs

    q, k, v = _mla_prep(zm, qn_ref, kvn_ref, wuq_ref, wukv_ref, cb_ref[...], su_ref[...], sd_ref[...])
    q_out[...] = q
    k_out[...] = k
    v_out[...] = v
    yb_out[...] = _spatial_gating(zg, sgln_ref, sgw_ref, sgb_ref)

    st = hg_st[...]
    for n in range(TOKEN_TILE // HG_TILE):
        rows = slice(n * HG_TILE, (n + 1) * HG_TILE)
        y, st = _hgrn2(zh[rows], lb_ref, hgn_ref, mcum3_ref, ind_ref, avg3_ref, st)
        yc_out[rows, :] = y
    hg_st[...] = st

    st = ret_st[...]
    cos, sin = cos_ref[...], sin_ref[...]
    for n in range(TOKEN_TILE // RET_CHUNK):
        rows = slice(n * RET_CHUNK, (n + 1) * RET_CHUNK)
        y, st = _retention(zr[rows], cos[rows], sin[rows], dec_ref, gq_ref, gk_ref, gc_ref, retn_ref,
                           avg3_ref, st)
        yd_out[rows, :] = y
    ret_st[...] = st


def _mix_call(h, tables, weights, S):
    T = h.shape[0]
    tm = TOKEN_TILE
    tok = lambda w: pl.BlockSpec((tm, w), lambda i: (i, 0))
    out_w = (D_MODEL, MLA_W, MLA_W, MLA_W, GROUP_WIDTH, GROUP_WIDTH, GROUP_WIDTH)
    out_t = (F32,) + (BF16,) * 6
    return pl.pallas_call(
        functools.partial(_mix_kernel, S // tm),
        grid=(T // tm,),
        in_specs=[tok(D_MODEL)] + [tok(LANE)] * 5 + [_weight_spec(w) for w in weights],
        out_specs=[tok(w) for w in out_w],
        out_shape=[jax.ShapeDtypeStruct((T, w), t) for w, t in zip(out_w, out_t)],
        scratch_shapes=[pltpu.VMEM((GROUP_WIDTH, HG_KW), F32), pltpu.VMEM((GROUP_WIDTH, GROUP_WIDTH), F32)],
        compiler_params=_params("arbitrary"),
        name="ffn_inproj_mixers",
    )(h, *tables, *[_weight_arg(w) for w in weights])


def _post_kernel(final, h_ref, ya_ref, yb_ref, yc_ref, yd_ref, p_ref, wo_ref, g_ref, wg_ref, wu_ref,
                 wd_ref, gp_ref, wpp_ref, wpg_ref, gf_ref, out_ref):
    h = h_ref[...]
    for n, y_ref in enumerate((ya_ref, yb_ref, yc_ref, yd_ref)):
        h = h + _dot(y_ref[...], wo_ref[n * GROUP_WIDTH:(n + 1) * GROUP_WIDTH, :])
    h = _ffn_half_step(h, g_ref, wg_ref, wu_ref, wd_ref)
    gate = jax.nn.sigmoid(_dot(_rms(h, gp_ref[...]).astype(BF16), wpg_ref[...]))
    h = h + _dot(p_ref[...].astype(BF16), wpp_ref[...]) * gate
    if final:
        h = _rms(h, gf_ref[...])
    out_ref[...] = h


def _post_call(h, ys, p, weights, final):
    T = h.shape[0]
    tm = TOKEN_TILE
    tok = lambda w: pl.BlockSpec((tm, w), lambda i: (i, 0))
    per_seq = p.stacked.shape[2] // tm
    p_spec = pl.BlockSpec((None, None, tm, PLE_DIM), lambda i: (p.index, i // per_seq, i % per_seq, 0))
    return pl.pallas_call(
        functools.partial(_post_kernel, final),
        grid=(T // tm,),
        in_specs=[tok(D_MODEL)] + [tok(GROUP_WIDTH)] * 4 + [p_spec] + [_weight_spec(w) for w in weights],
        out_specs=tok(D_MODEL),
        out_shape=jax.ShapeDtypeStruct((T, D_MODEL), F32),
        compiler_params=_params("parallel"),
        name="post_outproj_ffn_ple",
    )(h, *ys, p.stacked, *[_weight_arg(w) for w in weights])


def _prep_w_in(w):
    w = w.astype(BF16)
    lat = MLA_Q_LORA + MLA_KV_LORA
    pe = jnp.pad(w[..., lat:lat + MLA_ROPE], ((0, 0), (0, 0), (MLA_NOPE, MLA_HEAD_PAD - MLA_NOPE - MLA_ROPE)))
    ret = lat + MLA_ROPE + ZS_W + ZH_W
    half = RET_QK // 2
    src = np.arange(N_HEADS * RET_QK)
    dst = (src % RET_QK) // half * (N_HEADS * half) + src // RET_QK * half + src % half
    perm = np.zeros((src.size, src.size), np.float32)
    perm[src, dst] = 1.0
    perm = jnp.asarray(perm, dtype=BF16)
    regroup = lambda m: jnp.einsum('ldc,ce->lde', m, perm, preferred_element_type=BF16)
    rq, rk = w[..., ret:ret + 256], w[..., ret + 256:ret + 512]
    return jnp.concatenate([w[..., :lat], pe, w[..., lat + MLA_ROPE:ret], regroup(rq), regroup(rk),
                            w[..., ret + 512:]], axis=-1)


def _prep_w_uq(w):
    w = w.reshape(MLA_Q_LORA, N_HEADS, MLA_NOPE + MLA_ROPE)
    w = jnp.pad(w, ((0, 0), (0, 0), (0, MLA_HEAD_PAD - MLA_NOPE - MLA_ROPE)))
    return w.reshape(MLA_Q_LORA, MLA_W).astype(BF16)


def _prep_w_ukv(w):
    w = w.reshape(MLA_KV_LORA, N_HEADS, MLA_NOPE + HEAD_V)
    pad = lambda m: jnp.pad(m, ((0, 0), (0, 0), (0, MLA_HEAD_PAD - m.shape[-1]))).reshape(MLA_KV_LORA, -1)
    return jnp.concatenate([pad(w[..., :MLA_NOPE]), pad(w[..., MLA_NOPE:])], axis=1).astype(BF16)


def _rope_tables(positions):
    pos = positions.astype(F32).reshape(-1)[:, None]

    def inv_freq(d):
        return ROPE_THETA ** (-jnp.arange(0, d, 2, dtype=F32) / d)

    half = MLA_ROPE // 2
    lane = np.arange(MLA_HEAD_PAD)
    x1 = ((lane >= MLA_NOPE) & (lane < MLA_NOPE + half)).astype(np.float32)
    x2 = ((lane >= MLA_NOPE + half) & (lane < MLA_NOPE + MLA_ROPE)).astype(np.float32)
    nope = (lane < MLA_NOPE).astype(np.float32)
    freq = jnp.tile(inv_freq(RET_QK), N_HEADS)
    freq = freq.at[MLA_NOPE:MLA_NOPE + half].set(inv_freq(MLA_ROPE))
    freq = freq.at[MLA_NOPE + half:MLA_NOPE + MLA_ROPE].set(inv_freq(MLA_ROPE))
    ang = pos * freq[None, :]
    c, s = jnp.cos(ang), jnp.sin(ang)
    cb = c * (x1 + x2)[None, :] + nope[None, :]
    su = s * x2[None, :]
    sd = -s * x1[None, :]
    in_mla = jnp.asarray((x1 + x2) > 0)[None, :]
    cos_r = jnp.where(in_mla, jnp.roll(c, RET_QK // 2, axis=1), c)
    sin_r = jnp.where(in_mla, jnp.roll(s, RET_QK // 2, axis=1), s)
    return cb, su, sd, cos_r, sin_r


def kernel(x, p, positions, ffn1_norm, ffn1_w_gate, ffn1_w_up, ffn1_w_down, mix_norm, w_in, mla_q_norm, mla_w_uq, mla_kv_norm, mla_w_ukv, sg_ln, sg_w_s, sg_b_s, hg_lb_logits, hg_norm, ret_norm, w_out, ffn2_norm, ffn2_w_gate, ffn2_w_up, ffn2_w_down, ple_norm, ple_w_proj, ple_w_gate, final_norm):
    B, S, D = x.shape
    L = w_in.shape[0]
    T = B * S
    row = lambda a: a.reshape(1, -1).astype(F32)

    tables = _rope_tables(positions)
    mcum3, ind = _hg_constants()
    dec, gq, gk, gc = _ret_constants()
    avg3 = _head_avg3()
    lb_all = jnp.cumsum(jax.nn.softmax(hg_lb_logits.astype(F32), axis=0), axis=0)
    lb_all = lb_all - lb_all[0:1]
    gf = row(final_norm)
    bf16 = lambda a: a.astype(BF16)
    ffn1, ffn2 = [tuple(bf16(w) for w in ws) for ws in ((ffn1_w_gate, ffn1_w_up, ffn1_w_down),
                                                       (ffn2_w_gate, ffn2_w_up, ffn2_w_down))]
    win_all, wo_all, wpp_all, wpg_all = _prep_w_in(w_in), bf16(w_out), bf16(ple_w_proj), bf16(ple_w_gate)

    h = x.reshape(T, D)
    for l in range(L):
        sg_bias = jnp.repeat(sg_b_s[l].T, HEAD_V, axis=1)
        weights = (row(ffn1_norm[l]), *[_Layer(w, l) for w in ffn1],
                   row(mix_norm[l]), _Layer(win_all, l),
                   row(mla_q_norm[l]), row(mla_kv_norm[l]), _prep_w_uq(mla_w_uq[l]), _prep_w_ukv(mla_w_ukv[l]),
                   row(sg_ln[l]), sg_w_s[l], sg_bias,
                   row(lb_all[l]), row(hg_norm[l]), mcum3, ind, avg3,
                   dec, gq, gk, gc, row(ret_norm[l]))
        h, q, k, v, y_b, y_c, y_d = _mix_call(h, tables, weights, S)
        y_a = _attn_call(q, k, v, B, S)
        weights = (_Layer(wo_all, l), row(ffn2_norm[l]), *[_Layer(w, l) for w in ffn2],
                   row(ple_norm[l]), _Layer(wpp_all, l), _Layer(wpg_all, l), gf)
        h = _post_call(h, (y_a, y_b, y_c, y_d), _Layer(p, l), weights, final=(l == L - 1))
    return h.reshape(B, S, D)
```

```python
import functools
from typing import NamedTuple

import numpy as np
import jax
import jax.numpy as jnp
from jax import lax
from jax.experimental import pallas as pl
from jax.experimental.pallas import tpu as pltpu

F32 = jnp.float32
BF16 = jnp.bfloat16
EPS = 1e-6
ROPE_THETA = 10000.0

LANE = 128
D_MODEL = 1024
D_FF = 2816
FF_CHUNK = 256
N_FF_CHUNKS = D_FF // FF_CHUNK
PLE_DIM = 256
GROUP_WIDTH = 256
N_HEADS = 4
HEAD_V = GROUP_WIDTH // N_HEADS

MLA_NOPE, MLA_ROPE, MLA_Q_LORA, MLA_KV_LORA = 64, 32, 256, 128
MLA_HEAD_PAD = 128
MLA_W = N_HEADS * MLA_HEAD_PAD
SG_CHUNK = 128
HG_KEY = 128
HG_KW = N_HEADS * HG_KEY
HG_SUB = 8
HG_GROUP = 64
HG_TILE = 256
RET_QK = 64
RET_CHUNK = 256

ZM_W, ZS_W, ZH_W, ZR_W = 512, 512, 1536, 1024

TOKEN_TILE = 512
ATTN_Q_TILE = 1024
ATTN_KV_TILE = 512
ATTN_ROW_BLOCK = 32
VMEM_LIMIT = 60 * 1024 * 1024

NT_DIMS = (((1,), (1,)), ((), ()))
TN_DIMS = (((0,), (0,)), ((), ()))


def _rms(x, g):
    return x * lax.rsqrt(jnp.mean(x * x, axis=-1, keepdims=True) + EPS) * g


def _silu(x):
    return x * jax.nn.sigmoid(x)


def _dot(a, b):
    return jnp.dot(a, b, preferred_element_type=F32)


def _params(*sem):
    return pltpu.CompilerParams(dimension_semantics=sem, vmem_limit_bytes=VMEM_LIMIT)


def _full(shape):
    return pl.BlockSpec(shape, lambda *_: (0,) * len(shape))


class _Layer(NamedTuple):
    stacked: jax.Array
    index: int


def _weight_spec(w):
    if isinstance(w, _Layer):
        shape = w.stacked.shape[1:]
        return pl.BlockSpec((None,) + shape, lambda *_: (w.index,) + (0,) * len(shape),
                            pipeline_mode=pl.Buffered(1))
    return _full(w.shape)


def _weight_arg(w):
    return w.stacked if isinstance(w, _Layer) else w


def _split3(x):
    p1 = x.astype(BF16)
    r = x - p1.astype(F32)
    p2 = r.astype(BF16)
    p3 = (r - p2.astype(F32)).astype(BF16)
    return p1, p2, p3


def _sum_rows_f32(m3_ref, x):
    return _dot(m3_ref[...], jnp.concatenate(_split3(x), axis=0))


def _sum_lanes_f32(x, m3_ref):
    return _dot(jnp.concatenate(_split3(x), axis=1), m3_ref[...])


def _ffn_half_step(h, g_ref, wg_ref, wu_ref, wd_ref):
    x = _rms(h, g_ref[...]).astype(BF16)
    acc = jnp.zeros_like(h)
    for c in range(N_FF_CHUNKS):
        cols = slice(c * FF_CHUNK, (c + 1) * FF_CHUNK)
        a = (_silu(_dot(x, wg_ref[:, cols])) * _dot(x, wu_ref[:, cols])).astype(BF16)
        acc = acc + _dot(a, wd_ref[cols, :])
    return h + 0.5 * acc


def _rope_lanes(x, cb, su, sd):
    w = x.shape[1]
    half = MLA_ROPE // 2
    return x * cb + pltpu.roll(x, half, 1) * su + pltpu.roll(x, w - half, 1) * sd


def _mla_prep(z, qn_ref, kvn_ref, wuq_ref, wukv_ref, cb, su, sd):
    c_q = z[:, :MLA_Q_LORA]
    c_kv = z[:, MLA_Q_LORA:MLA_Q_LORA + MLA_KV_LORA]
    k_pe = z[:, MLA_Q_LORA + MLA_KV_LORA:]
    q = _dot(_rms(c_q, qn_ref[...]).astype(BF16), wuq_ref[...])
    kv = _dot(_rms(c_kv, kvn_ref[...]).astype(BF16), wukv_ref[...])
    tile4 = lambda t: jnp.concatenate([t] * N_HEADS, axis=1)
    scale = (MLA_NOPE + MLA_ROPE) ** -0.5
    q = (_rope_lanes(q, tile4(cb), tile4(su), tile4(sd)) * scale).astype(BF16)
    k = (kv[:, :MLA_W] + tile4(_rope_lanes(k_pe, cb, su, sd))).astype(BF16)
    return q, k, kv[:, MLA_W:].astype(BF16)


def _attn_kernel(q_ref, k_ref, v_ref, o_ref, s_ref, p_ref, m_ref, l_ref, a_ref, acc_ref):
    i = pl.program_id(1)
    tq, tk, rb = ATTN_Q_TILE, ATTN_KV_TILE, ATTN_ROW_BLOCK
    head_lanes = [slice(h * MLA_HEAD_PAD, (h + 1) * MLA_HEAD_PAD) for h in range(N_HEADS)]
    m_ref[...] = jnp.full(m_ref.shape, -jnp.inf, F32)
    l_ref[...] = jnp.zeros(l_ref.shape, F32)
    acc_ref[...] = jnp.zeros(acc_ref.shape, F32)
    row = lax.broadcasted_iota(jnp.int32, (rb, tk), 0)
    col = lax.broadcasted_iota(jnp.int32, (rb, tk), 1)

    def step(j, shift):
        start = pl.multiple_of(j * tk, tk)
        first = 0 if shift is None else shift
        live = slice(first, tq)
        for h, lanes in enumerate(head_lanes):
            s_ref[h, live, :] = lax.dot_general(q_ref[live, lanes], k_ref[pl.ds(start, tk), lanes], NT_DIMS,
                                                preferred_element_type=F32)
            for r in range(first // rb, tq // rb):
                rows = slice(r * rb, (r + 1) * rb)
                s = s_ref[h, rows, :]
                if shift is not None and r * rb < shift + tk - 1:
                    s = jnp.where(row + r * rb >= col + shift, s, -jnp.inf)
                tiles = [s[:, c * LANE:(c + 1) * LANE] for c in range(tk // LANE)]
                m_old = m_ref[h, rows, :]
                m_new = jnp.maximum(m_old, jnp.max(functools.reduce(jnp.maximum, tiles), axis=1, keepdims=True))
                p_tiles = [jnp.exp(t - m_new) for t in tiles]
                alpha = jnp.exp(m_old - m_new)
                l_ref[h, rows, :] = alpha * l_ref[h, rows, :] + jnp.sum(functools.reduce(jnp.add, p_tiles),
                                                                      axis=1, keepdims=True)
                m_ref[h, rows, :] = m_new
                a_ref[h, rows, :] = alpha
                p_ref[h, rows, :] = jnp.concatenate([t.astype(BF16) for t in p_tiles], axis=1)
            acc_ref[h, live, :] = (a_ref[h, live, :] * acc_ref[h, live, :]
                                   + _dot(p_ref[h, live, :], v_ref[pl.ds(start, tk), lanes]))

    per_q = tq // tk

    def body(j, carry):
        step(j, None)
        return carry

    lax.fori_loop(0, i * per_q, body, 0)
    for d in range(per_q):
        step(i * per_q + d, d * tk)
    heads = [acc_ref[h] / l_ref[h] for h in range(N_HEADS)]
    lane = lax.broadcasted_iota(jnp.int32, (tq, MLA_HEAD_PAD), 1)
    lo = lane < HEAD_V
    pair = lambda a, b: jnp.where(lo, a, pltpu.roll(b, HEAD_V, 1))
    o_ref[...] = jnp.concatenate([pair(heads[0], heads[1]), pair(heads[2], heads[3])],
                                 axis=1).astype(o_ref.dtype)


def _attn_call(q, k, v, B, S):
    nq = S // ATTN_Q_TILE
    tq, tk = ATTN_Q_TILE, ATTN_KV_TILE
    stat = pltpu.VMEM((N_HEADS, tq, LANE), F32)
    return pl.pallas_call(
        _attn_kernel,
        grid=(B, nq),
        in_specs=[pl.BlockSpec((tq, MLA_W), lambda b, i: (b * nq + i, 0)),
                  pl.BlockSpec((S, MLA_W), lambda b, i: (b, 0)),
                  pl.BlockSpec((S, MLA_W), lambda b, i: (b, 0))],
        out_specs=pl.BlockSpec((tq, GROUP_WIDTH), lambda b, i: (b * nq + i, 0)),
        out_shape=jax.ShapeDtypeStruct((B * S, GROUP_WIDTH), BF16),
        scratch_shapes=[pltpu.VMEM((N_HEADS, tq, tk), F32), pltpu.VMEM((N_HEADS, tq, tk), BF16),
                        stat, stat, stat, stat],
        compiler_params=_params("parallel", "arbitrary"),
        name="mla_attention",
    )(q, k, v)


def _gelu(x):
    return 0.5 * x * (1.0 + lax.erf(x * np.float32(np.sqrt(0.5))))


def _spatial_gating(z, ln_ref, w_ref, bias_ref):
    u = _gelu(z[:, :GROUP_WIDTH])
    gv = _gelu(z[:, GROUP_WIDTH:])
    xc = gv - jnp.mean(gv, axis=-1, keepdims=True)
    v = xc * lax.rsqrt(jnp.mean(xc * xc, axis=-1, keepdims=True) + EPS) * ln_ref[...]
    C = SG_CHUNK
    causal = (lax.broadcasted_iota(jnp.int32, (C, C), 0) >= lax.broadcasted_iota(jnp.int32, (C, C), 1))
    head = lax.broadcasted_iota(jnp.int32, (1, GROUP_WIDTH), 1) // HEAD_V
    ws = [jnp.where(causal, w_ref[h], 0.0).astype(BF16) for h in range(N_HEADS)]
    bias = bias_ref[...]
    out = []
    for c in range(z.shape[0] // C):
        rows = slice(c * C, (c + 1) * C)
        vc = v[rows]
        mixed = bias
        for h in range(N_HEADS):
            mixed = mixed + _dot(ws[h], jnp.where(head == h, vc, 0.0).astype(BF16))
        out.append((u[rows] * mixed).astype(BF16))
    return jnp.concatenate(out, axis=0)


def _hg_constants():
    t = np.arange(HG_TILE)
    same = (t[:, None] // HG_GROUP) == (t[None, :] // HG_GROUP)
    m_cum = (same & (t[None, :] <= t[:, None])).astype(np.float32)
    kg = np.arange(HG_KW)
    e = np.arange(GROUP_WIDTH)
    ind = (kg[:, None] // HG_KEY == e[None, :] // HEAD_V).astype(np.float32)
    return jnp.asarray(np.tile(m_cum, (1, 3)), dtype=BF16), jnp.asarray(ind, dtype=BF16)


def _head_avg3():
    e = np.arange(GROUP_WIDTH)
    avg = (e[:, None] // HEAD_V == e[None, :] // HEAD_V).astype(np.float32) / HEAD_V
    return jnp.asarray(np.tile(avg, (3, 1)), dtype=BF16)


def _hgrn2(z, lb_ref, ng_ref, mcum3_ref, ind_ref, avg3_ref, st):
    TT, KW = HG_TILE, HG_KW
    n_grp, n_sub = TT // HG_GROUP, HG_GROUP // HG_SUB
    q = z[:, :KW]
    f = z[:, KW:2 * KW]
    vi = z[:, 2 * KW:2 * KW + GROUP_WIDTH]
    g = z[:, 2 * KW + GROUP_WIDTH:]
    lb = lb_ref[...]
    forget = lb + (1.0 - lb) * jax.nn.sigmoid(f)
    kk = 1.0 - forget
    b = _sum_rows_f32(mcum3_ref, jnp.log(forget))
    head_v = lax.broadcasted_iota(jnp.int32, (1, GROUP_WIDTH), 1) // HEAD_V
    ind = ind_ref[...]

    tiles = lambda a: a.reshape(TT // HG_SUB, HG_SUB, a.shape[-1])
    q3, f3, v3 = tiles(q), tiles(forget), tiles(vi)
    r_sub = lax.broadcasted_iota(jnp.int32, (1, HG_SUB, 1), 1)
    o = _dot((q * kk).astype(BF16), ind) * vi
    decay, f_prev = None, f3
    for d in range(1, HG_SUB):
        f_d = pltpu.roll(f3, d, 1)
        decay = f3 if d == 1 else decay * f_prev
        x = jnp.where(r_sub >= d, q3 * (1.0 - f_d) * decay, 0.0)
        x2 = x.reshape(TT, KW)
        sums = [jnp.sum(x2[:, h * HG_KEY:(h + 1) * HG_KEY], axis=1, keepdims=True) for h in range(N_HEADS)]
        w = jnp.where(head_v < 2, jnp.where(head_v == 0, sums[0], sums[1]),
                      jnp.where(head_v == 2, sums[2], sums[3]))
        o = o + w * pltpu.roll(v3, d, 1).reshape(TT, GROUP_WIDTH)
        f_prev = f_d

    groups = lambda a: a.reshape(n_grp, n_sub, HG_SUB, a.shape[-1])
    b4, kk4, q4 = groups(b), groups(kk), groups(q)
    ends = jnp.broadcast_to(b4[:, :, HG_SUB - 1:HG_SUB, :], b4.shape)
    bsub4 = jnp.concatenate([jnp.zeros_like(ends[:, :1]), ends[:, :n_sub - 1]], axis=1)
    blast4 = ends[:, n_sub - 1:]

    qp4 = q4 * jnp.exp(b4 - bsub4)
    zeros4 = lambda n: [jnp.zeros((n_grp, n, HG_SUB, KW), F32)] if n else []
    q_slots, k_slots = [], []
    for i in range(1, n_sub):
        k_i = kk4[:, :i] * jnp.exp(bsub4[:, i:i + 1] - b4[:, :i])
        k_slots.append(jnp.concatenate([k_i] + zeros4(n_sub - i), axis=1).reshape(TT, KW))
        q_slots.append(jnp.concatenate(zeros4(i) + [qp4[:, i:i + 1]] + zeros4(n_sub - 1 - i),
                                       axis=1).reshape(TT, KW))
    same_group = ((lax.broadcasted_iota(jnp.int32, (TT, TT), 0) // HG_GROUP)
                  == (lax.broadcasted_iota(jnp.int32, (TT, TT), 1) // HG_GROUP))
    for h in range(N_HEADS):
        lanes = slice(h * HG_KEY, (h + 1) * HG_KEY)
        lhs = jnp.concatenate([s[:, lanes] for s in q_slots], axis=1).astype(BF16)
        rhs = jnp.concatenate([s[:, lanes] for s in k_slots], axis=1).astype(BF16)
        a = lax.dot_general(lhs, rhs, NT_DIMS, preferred_element_type=F32)
        a = jnp.where(same_group, a, 0.0).astype(BF16)
        o = o + _dot(a, jnp.where(head_v == h, vi, 0.0).astype(BF16))

    qe = (q * jnp.exp(b)).astype(BF16)
    k_dec = (kk4 * jnp.exp(blast4 - b4)).reshape(TT, KW).astype(BF16)
    e_last = jnp.exp(blast4[:, 0, 0:1, :])
    vb = vi.astype(BF16)
    same_head = (lax.broadcasted_iota(jnp.int32, (GROUP_WIDTH, 1), 0) // HEAD_V
                 == lax.broadcasted_iota(jnp.int32, (1, KW), 1) // HG_KEY)
    inter = []
    for gi in range(n_grp):
        rows = slice(gi * HG_GROUP, (gi + 1) * HG_GROUP)
        inter.append(lax.dot_general(qe[rows], st.astype(BF16), NT_DIMS, preferred_element_type=F32))
        upd = lax.dot_general(vb[rows], k_dec[rows], TN_DIMS, preferred_element_type=F32)
        st = st * e_last[gi] + jnp.where(same_head, upd, 0.0)
    o = o + jnp.concatenate(inter, axis=0)

    ms = _sum_lanes_f32(o * o, avg3_ref)
    return (o * lax.rsqrt(ms + EPS) * ng_ref[...] * _silu(g)).astype(BF16), st


def _ret_constants():
    C = RET_CHUNK
    lg = np.log1p(-np.power(2.0, -(5.0 + np.arange(N_HEADS, dtype=np.float64))))
    pos = np.arange(C, dtype=np.float64)
    rel = pos[:, None] - pos[None, :]
    decay = np.where(rel >= 0, np.exp(np.maximum(rel, 0.0)[None] * lg[:, None, None]), 0.0)
    lane = np.arange(GROUP_WIDTH)
    lg_qk = lg[(lane % 128) // (RET_QK // 2)]
    lg_v = lg[lane // HEAD_V]
    gq = np.exp((pos + 1.0)[:, None] * lg_qk[None, :])
    gk = np.exp((C - 1.0 - pos)[:, None] * lg_qk[None, :])
    gc = np.exp(C * lg_v)[None, :]
    f = lambda a: jnp.asarray(a.astype(np.float32))
    return f(decay), f(gq), f(gk), f(gc)


def _retention(z, c, s, dec_ref, gq_ref, gk_ref, gc_ref, ng_ref, avg3_ref, st):
    W = GROUP_WIDTH
    q1, q2 = z[:, 0:128], z[:, 128:256]
    k1, k2 = z[:, 256:384], z[:, 384:512]
    v = z[:, 2 * W:3 * W]
    g = z[:, 3 * W:]
    qr = jnp.concatenate([q1 * c - q2 * s, q1 * s + q2 * c], axis=1)
    kr = jnp.concatenate([k1 * c - k2 * s, k1 * s + k2 * c], axis=1) * (RET_QK ** -0.5)
    lane = lax.broadcasted_iota(jnp.int32, (1, W), 1)
    head_qk = (lane & 127) // (RET_QK // 2)
    head_v = lane // HEAD_V
    krb = kr.astype(BF16)
    vb = v.astype(BF16)

    o = _dot((qr * gq_ref[...]).astype(BF16), st.astype(BF16))
    for h in range(N_HEADS):
        qm = jnp.where(head_qk == h, qr, 0.0).astype(BF16)
        sc = lax.dot_general(qm, krb, NT_DIMS, preferred_element_type=F32) * dec_ref[h]
        o = o + _dot(sc.astype(BF16), jnp.where(head_v == h, vb, jnp.zeros_like(vb)))

    upd = lax.dot_general((kr * gk_ref[...]).astype(BF16), vb, TN_DIMS, preferred_element_type=F32)
    row_head = (lax.broadcasted_iota(jnp.int32, (W, 1), 0) & 127) // (RET_QK // 2)
    st = st * gc_ref[...] + jnp.where(row_head == head_v, upd, 0.0)

    oc = o - _sum_lanes_f32(o, avg3_ref)
    var = _sum_lanes_f32(oc * oc, avg3_ref)
    return (oc * lax.rsqrt(var + EPS) * ng_ref[...] * _silu(g)).astype(BF16), st


def _mix_kernel(tiles_per_seq,
                h_ref, cb_ref, su_ref, sd_ref, cos_ref, sin_ref,
                g1_ref, wg_ref, wu_ref, wd_ref, g2_ref, win_ref,
                qn_ref, kvn_ref, wuq_ref, wukv_ref,
                sgln_ref, sgw_ref, sgb_ref,
                lb_ref, hgn_ref, mcum3_ref, ind_ref, avg3_ref,
                dec_ref, gq_ref, gk_ref, gc_ref, retn_ref,
                h_out, q_out, k_out, v_out, yb_out, yc_out, yd_out,
                hg_st, ret_st):
    @pl.when(pl.program_id(0) % tiles_per_seq == 0)
    def _():
        hg_st[...] = jnp.zeros_like(hg_st)
        ret_st[...] = jnp.zeros_like(ret_st)

    h1 = _ffn_half_step(h_ref[...], g1_ref, wg_ref, wu_ref, wd_ref)
    h_out[...] = h1
    x = _rms(h1, g2_ref[...]).astype(BF16)
    zs = []
    off = 0
    for w in (ZM_W, ZS_W, ZH_W, ZR_W):
        zs.append(_dot(x, win_ref[:, off:off + w]))
        off += w
    zm, zg, zh, zr = zs

    q, k, v = _mla_prep(zm, qn_ref, kvn_ref, wuq_ref, wukv_ref, cb_ref[...], su_ref[...], sd_ref[...])
    q_out[...] = q
    k_out[...] = k
    v_out[...] = v
    yb_out[...] = _spatial_gating(zg, sgln_ref, sgw_ref, sgb_ref)

    st = hg_st[...]
    for n in range(TOKEN_TILE // HG_TILE):
        rows = slice(n * HG_TILE, (n + 1) * HG_TILE)
        y, st = _hgrn2(zh[rows], lb_ref, hgn_ref, mcum3_ref, ind_ref, avg3_ref, st)
        yc_out[rows, :] = y
    hg_st[...] = st

    st = ret_st[...]
    cos, sin = cos_ref[...], sin_ref[...]
    for n in range(TOKEN_TILE // RET_CHUNK):
        rows = slice(n * RET_CHUNK, (n + 1) * RET_CHUNK)
        y, st = _retention(zr[rows], cos[rows], sin[rows], dec_ref, gq_ref, gk_ref, gc_ref, retn_ref,
                           avg3_ref, st)
        yd_out[rows, :] = y
    ret_st[...] = st


def _mix_call(h, tables, weights, S):
    T = h.shape[0]
    tm = TOKEN_TILE
    tok = lambda w: pl.BlockSpec((tm, w), lambda i: (i, 0))
    out_w = (D_MODEL, MLA_W, MLA_W, MLA_W, GROUP_WIDTH, GROUP_WIDTH, GROUP_WIDTH)
    out_t = (F32,) + (BF16,) * 6
    return pl.pallas_call(
        functools.partial(_mix_kernel, S // tm),
        grid=(T // tm,),
        in_specs=[tok(D_MODEL)] + [tok(LANE)] * 5 + [_weight_spec(w) for w in weights],
        out_specs=[tok(w) for w in out_w],
        out_shape=[jax.ShapeDtypeStruct((T, w), t) for w, t in zip(out_w, out_t)],
        scratch_shapes=[pltpu.VMEM((GROUP_WIDTH, HG_KW), F32), pltpu.VMEM((GROUP_WIDTH, GROUP_WIDTH), F32)],
        compiler_params=_params("arbitrary"),
        name="ffn_inproj_mixers",
    )(h, *tables, *[_weight_arg(w) for w in weights])


def _post_kernel(final, h_ref, ya_ref, yb_ref, yc_ref, yd_ref, p_ref, wo_ref, g_ref, wg_ref, wu_ref,
                 wd_ref, gp_ref, wpp_ref, wpg_ref, gf_ref, out_ref):
    h = h_ref[...]
    for n, y_ref in enumerate((ya_ref, yb_ref, yc_ref, yd_ref)):
        h = h + _dot(y_ref[...], wo_ref[n * GROUP_WIDTH:(n + 1) * GROUP_WIDTH, :])
    h = _ffn_half_step(h, g_ref, wg_ref, wu_ref, wd_ref)
    gate = jax.nn.sigmoid(_dot(_rms(h, gp_ref[...]).astype(BF16), wpg_ref[...]))
    h = h + _dot(p_ref[...].astype(BF16), wpp_ref[...]) * gate
    if final:
        h = _rms(h, gf_ref[...])
    out_ref[...] = h


def _post_call(h, ys, p, weights, final):
    T = h.shape[0]
    tm = TOKEN_TILE
    tok = lambda w: pl.BlockSpec((tm, w), lambda i: (i, 0))
    p_spec = pl.BlockSpec((None, tm, PLE_DIM), lambda i: (p.index, i, 0))
    return pl.pallas_call(
        functools.partial(_post_kernel, final),
        grid=(T // tm,),
        in_specs=[tok(D_MODEL)] + [tok(GROUP_WIDTH)] * 4 + [p_spec] + [_weight_spec(w) for w in weights],
        out_specs=tok(D_MODEL),
        out_shape=jax.ShapeDtypeStruct((T, D_MODEL), F32),
        compiler_params=_params("parallel"),
        name="post_outproj_ffn_ple",
    )(h, *ys, p.stacked, *[_weight_arg(w) for w in weights])


def _prep_w_in(w):
    w = w.astype(BF16)
    lat = MLA_Q_LORA + MLA_KV_LORA
    pe = jnp.pad(w[..., lat:lat + MLA_ROPE], ((0, 0), (0, 0), (MLA_NOPE, MLA_HEAD_PAD - MLA_NOPE - MLA_ROPE)))
    ret = lat + MLA_ROPE + ZS_W + ZH_W
    half = RET_QK // 2
    src = np.arange(N_HEADS * RET_QK)
    dst = (src % RET_QK) // half * (N_HEADS * half) + src // RET_QK * half + src % half
    perm = np.zeros((src.size, src.size), np.float32)
    perm[src, dst] = 1.0
    perm = jnp.asarray(perm, dtype=BF16)
    regroup = lambda m: jnp.einsum('ldc,ce->lde', m, perm, preferred_element_type=BF16)
    rq, rk = w[..., ret:ret + 256], w[..., ret + 256:ret + 512]
    return jnp.concatenate([w[..., :lat], pe, w[..., lat + MLA_ROPE:ret], regroup(rq), regroup(rk),
                            w[..., ret + 512:]], axis=-1)


def _prep_w_uq(w):
    w = w.reshape(MLA_Q_LORA, N_HEADS, MLA_NOPE + MLA_ROPE)
    w = jnp.pad(w, ((0, 0), (0, 0), (0, MLA_HEAD_PAD - MLA_NOPE - MLA_ROPE)))
    return w.reshape(MLA_Q_LORA, MLA_W).astype(BF16)


def _prep_w_ukv(w):
    w = w.reshape(MLA_KV_LORA, N_HEADS, MLA_NOPE + HEAD_V)
    pad = lambda m: jnp.pad(m, ((0, 0), (0, 0), (0, MLA_HEAD_PAD - m.shape[-1]))).reshape(MLA_KV_LORA, -1)
    return jnp.concatenate([pad(w[..., :MLA_NOPE]), pad(w[..., MLA_NOPE:])], axis=1).astype(BF16)


def _rope_tables(positions):
    pos = positions.astype(F32).reshape(-1)[:, None]

    def inv_freq(d):
        return ROPE_THETA ** (-jnp.arange(0, d, 2, dtype=F32) / d)

    half = MLA_ROPE // 2
    lane = np.arange(MLA_HEAD_PAD)
    x1 = ((lane >= MLA_NOPE) & (lane < MLA_NOPE + half)).astype(np.float32)
    x2 = ((lane >= MLA_NOPE + half) & (lane < MLA_NOPE + MLA_ROPE)).astype(np.float32)
    nope = (lane < MLA_NOPE).astype(np.float32)
    freq = jnp.tile(inv_freq(RET_QK), N_HEADS)
    freq = freq.at[MLA_NOPE:MLA_NOPE + half].set(inv_freq(MLA_ROPE))
    freq = freq.at[MLA_NOPE + half:MLA_NOPE + MLA_ROPE].set(inv_freq(MLA_ROPE))
    ang = pos * freq[None, :]
    c, s = jnp.cos(ang), jnp.sin(ang)
    cb = c * (x1 + x2)[None, :] + nope[None, :]
    su = s * x2[None, :]
    sd = -s * x1[None, :]
    in_mla = jnp.asarray((x1 + x2) > 0)[None, :]
    cos_r = jnp.where(in_mla, jnp.roll(c, RET_QK // 2, axis=1), c)
    sin_r = jnp.where(in_mla, jnp.roll(s, RET_QK // 2, axis=1), s)
    return cb, su, sd, cos_r, sin_r


def kernel(x, p, positions, ffn1_norm, ffn1_w_gate, ffn1_w_up, ffn1_w_down, mix_norm, w_in, mla_q_norm, mla_w_uq, mla_kv_norm, mla_w_ukv, sg_ln, sg_w_s, sg_b_s, hg_lb_logits, hg_norm, ret_norm, w_out, ffn2_norm, ffn2_w_gate, ffn2_w_up, ffn2_w_down, ple_norm, ple_w_proj, ple_w_gate, final_norm):
    B, S, D = x.shape
    L = w_in.shape[0]
    T = B * S
    row = lambda a: a.reshape(1, -1).astype(F32)

    tables = _rope_tables(positions)
    mcum3, ind = _hg_constants()
    dec, gq, gk, gc = _ret_constants()
    avg3 = _head_avg3()
    lb_all = jnp.cumsum(jax.nn.softmax(hg_lb_logits.astype(F32), axis=0), axis=0)
    lb_all = lb_all - lb_all[0:1]
    gf = row(final_norm)
    bf16 = lambda a: a.astype(BF16)
    ffn1, ffn2 = [tuple(bf16(w) for w in ws) for ws in ((ffn1_w_gate, ffn1_w_up, ffn1_w_down),
                                                       (ffn2_w_gate, ffn2_w_up, ffn2_w_down))]
    win_all, wo_all, wpp_all, wpg_all = _prep_w_in(w_in), bf16(w_out), bf16(ple_w_proj), bf16(ple_w_gate)
    p_all = p.reshape(L, T, PLE_DIM)

    h = x.reshape(T, D)
    for l in range(L):
        sg_bias = jnp.repeat(sg_b_s[l].T, HEAD_V, axis=1)
        weights = (row(ffn1_norm[l]), *[_Layer(w, l) for w in ffn1],
                   row(mix_norm[l]), _Layer(win_all, l),
                   row(mla_q_norm[l]), row(mla_kv_norm[l]), _prep_w_uq(mla_w_uq[l]), _prep_w_ukv(mla_w_ukv[l]),
                   row(sg_ln[l]), sg_w_s[l], sg_bias,
                   row(lb_all[l]), row(hg_norm[l]), mcum3, ind, avg3,
                   dec, gq, gk, gc, row(ret_norm[l]))
        h, q, k, v, y_b, y_c, y_d = _mix_call(h, tables, weights, S)
        y_a = _attn_call(q, k, v, B, S)
        weights = (_Layer(wo_all, l), row(ffn2_norm[l]), *[_Layer(w, l) for w in ffn2],
                   row(ple_norm[l]), _Layer(wpp_all, l), _Layer(wpg_all, l), gf)
        h = _post_call(h, (y_a, y_b, y_c, y_d), _Layer(p_all, l), weights, final=(l == L - 1))
    return h.reshape(B, S, D)
```
